```python
import math
import jax, jax.numpy as jnp
from jax import lax
import numpy as np


D_MODEL = 1024
BATCH = 32
SEQ = 2048
DEPTH = 4
DEC_BATCH = 2
DEC_SEQ = 8192
PAST_LEN = 128

MIX_WIDTH = D_MODEL
MLA_HEADS = 8
MLA_NOPE = 64
MLA_ROPE = 32
MLA_V = 64
MLA_Q_RANK = D_MODEL // 4
MLA_KV_RANK = D_MODEL // 8
MLA_OUT = MLA_HEADS * MLA_V
ROPE_BASE = 10000.0
Q_BLOCK = 128
GM_WIDTH = D_MODEL // 4
GM_GROUPS = 4
GM_CHUNK = 128
HG_HEADS = 4
HG_DK = 64
HG_DV = 64
HG_KEYS = HG_HEADS * HG_DK
HG_WIDTH = HG_HEADS * HG_DV
HG_CHUNK = 32
F_FLOOR = 1e-20
D_FF = 4 * D_MODEL
EPS = 1e-6
IN_SIZES = (MLA_Q_RANK, MLA_KV_RANK, MLA_ROPE, GM_WIDTH, GM_WIDTH, HG_KEYS, HG_KEYS, HG_KEYS, HG_WIDTH, HG_WIDTH)
IN_COLS = sum(IN_SIZES)

kernel_name = 'hybrid_bidir_encoder'


def rms_norm(x, g):
    xf = x.astype(jnp.float32)
    y = xf * lax.rsqrt(jnp.mean(xf * xf, axis=-1, keepdims=True) + EPS)
    return (y * g.astype(jnp.float32)).astype(x.dtype)


def group_layer_norm(x, g, b, groups):
    shp = x.shape
    xf = x.astype(jnp.float32).reshape(shp[:-1] + (groups, shp[-1] // groups))
    mu = jnp.mean(xf, axis=-1, keepdims=True)
    xc = xf - mu
    y = (xc * lax.rsqrt(jnp.mean(xc * xc, axis=-1, keepdims=True) + EPS)).reshape(shp)
    return (y * g.astype(jnp.float32) + b.astype(jnp.float32)).astype(x.dtype)


def rope_tables(seq_len):
    half = MLA_ROPE // 2
    inv = ROPE_BASE ** (-(jnp.arange(half, dtype=jnp.float32) / half))
    ang = jnp.arange(seq_len, dtype=jnp.float32)[:, None] * inv[None, :]
    return jnp.cos(ang), jnp.sin(ang)


def apply_rope(x, cos, sin):
    half = x.shape[-1] // 2
    x1, x2 = x[..., :half], x[..., half:]
    cos = cos.astype(x.dtype)
    sin = sin.astype(x.dtype)
    return jnp.concatenate([x1 * cos - x2 * sin, x1 * sin + x2 * cos], axis=-1)


def mla_mixer(c_q, c_kv, k_rope, g_q_a, w_uq, g_kv_a, w_ukv):
    B, S, _ = c_q.shape
    q = (rms_norm(c_q, g_q_a) @ w_uq).reshape(B, S, MLA_HEADS, MLA_NOPE + MLA_ROPE)
    kv = (rms_norm(c_kv, g_kv_a) @ w_ukv).reshape(B, S, MLA_HEADS, MLA_NOPE + MLA_V)
    cos, sin = rope_tables(S)
    q_nope = q[..., :MLA_NOPE]
    q_pe = apply_rope(q[..., MLA_NOPE:], cos[:, None, :], sin[:, None, :])
    k_pe = apply_rope(k_rope, cos, sin)
    k_nope = kv[..., :MLA_NOPE]
    v = kv[..., MLA_NOPE:]
    scale = (MLA_NOPE + MLA_ROPE) ** -0.5
    nb = S // Q_BLOCK
    qn_blk = q_nope.reshape(B, nb, Q_BLOCK, MLA_HEADS, MLA_NOPE).transpose(1, 0, 2, 3, 4)
    qp_blk = q_pe.reshape(B, nb, Q_BLOCK, MLA_HEADS, MLA_ROPE).transpose(1, 0, 2, 3, 4)

    def attend(blk):
        qn, qp = blk
        s = (jnp.einsum('bqhd,bkhd->bhqk', qn, k_nope)
             + jnp.einsum('bqhr,bkr->bhqk', qp, k_pe)).astype(jnp.float32) * scale
        p = jax.nn.softmax(s, axis=-1).astype(v.dtype)
        return jnp.einsum('bhqk,bkhd->bqhd', p, v)

    o = lax.map(attend, (qn_blk, qp_blk))
    return o.transpose(1, 0, 2, 3, 4).reshape(B, S, MLA_OUT)


def gmlp_mixer(u, v, ln_g, ln_b, w_s, b_s):
    B, S, C = v.shape
    vn = group_layer_norm(v, ln_g, ln_b, GM_GROUPS)
    vc = vn.reshape(B, S // GM_CHUNK, GM_CHUNK, C)
    vs = jnp.einsum('ts,bnsc->bntc', w_s, vc) + b_s[:, None]
    return u * vs.reshape(B, S, C)


def hgrn_gates(f_logit, lb):
    xf = f_logit.astype(jnp.float32)
    k = (1.0 - lb) * jax.nn.sigmoid(-xf)
    f = lb + (1.0 - lb) * jax.nn.sigmoid(xf)
    log_f = jnp.log(jnp.maximum(f, F_FLOOR))
    return log_f, k


def gla_chunk_scan(q, k, v, log_f):
    B, S, H, DK = q.shape
    DV = v.shape[-1]
    n = S // HG_CHUNK

    def to_chunks(t):
        return t.reshape(B, n, HG_CHUNK, H, t.shape[-1]).transpose(1, 0, 3, 2, 4)

    mask = jnp.tril(jnp.ones((HG_CHUNK, HG_CHUNK), dtype=bool))[:, :, None]
    maskf = mask.astype(jnp.float32)

    def step(state, blk):
        qc, kc, vc, lf = blk
        b = jnp.cumsum(lf, axis=2)
        o_inter = jnp.einsum('bhtk,bhkv->bhtv', qc * jnp.exp(b), state)
        diff = b[:, :, :, None, :] - b[:, :, None, :, :]
        decay = jnp.exp(jnp.where(mask, diff, 0.0)) * maskf
        a = jnp.einsum('bhtk,bhsk,bhtsk->bhts', qc, kc, decay)
        o = o_inter + jnp.einsum('bhts,bhsv->bhtv', a, vc)
        b_last = b[:, :, -1:, :]
        k_dec = kc * jnp.exp(b_last - b)
        new_state = jnp.exp(b_last[:, :, 0, :])[..., None] * state + jnp.einsum('bhsk,bhsv->bhkv', k_dec, vc)
        return new_state, o

    init = jnp.zeros((B, H, DK, DV), jnp.float32)
    _, o = lax.scan(step, init, (to_chunks(q), to_chunks(k), to_chunks(v), to_chunks(log_f)))
    return o.transpose(1, 0, 3, 2, 4).reshape(B, S, H, DV)


def hgrn_mixer(q, f_fwd, f_bwd, inp, gate, lb, g_o):
    B, S, _ = q.shape
    qh = q.reshape(B, S, HG_HEADS, HG_DK).astype(jnp.float32)
    vh = inp.reshape(B, S, HG_HEADS, HG_DV).astype(jnp.float32)
    lbh = lb.astype(jnp.float32).reshape(2, HG_HEADS, HG_DK)
    logf_f, k_f = hgrn_gates(f_fwd.reshape(B, S, HG_HEADS, HG_DK), lbh[0])
    logf_b, k_b = hgrn_gates(f_bwd.reshape(B, S, HG_HEADS, HG_DK), lbh[1])
    o_f = gla_chunk_scan(qh, k_f, vh, logf_f)
    o_b = jnp.flip(gla_chunk_scan(jnp.flip(qh, 1), jnp.flip(k_b, 1), jnp.flip(vh, 1), jnp.flip(logf_b, 1)), 1)
    o = rms_norm(o_f + o_b, g_o.reshape(HG_HEADS, HG_DV))
    o = o * jax.nn.silu(gate.reshape(B, S, HG_HEADS, HG_DV).astype(jnp.float32))
    return o.reshape(B, S, HG_WIDTH).astype(q.dtype)


def encoder_trunk(x, hg_lb_logits, g_pre_mix, w_in, g_q_a, w_uq, g_kv_a, w_ukv, gm_ln_g, gm_ln_b,
                  gm_w_s, gm_b_s, hg_g_o, w_out, g_post_mix, g_pre_ffn, w_ff1, w_ff2, g_post_ffn):
    p = jax.nn.softmax(hg_lb_logits.astype(jnp.float32), axis=0)
    lb_all = jnp.clip(jnp.cumsum(p, axis=0) - p[0], 0.0, 0.999)
    splits = np.cumsum(IN_SIZES)[:-1].tolist()
    for l in range(DEPTH):
        h = rms_norm(x, g_pre_mix[l])
        z = h @ w_in[l]
        c_q, c_kv, k_rope, gm_u, gm_v, hg_q, hg_ff, hg_fb, hg_i, hg_g = jnp.split(z, splits, axis=-1)
        o_a = mla_mixer(c_q, c_kv, k_rope, g_q_a[l], w_uq[l], g_kv_a[l], w_ukv[l])
        o_b = gmlp_mixer(jax.nn.gelu(gm_u, approximate=False), jax.nn.gelu(gm_v, approximate=False),
                         gm_ln_g[l], gm_ln_b[l], gm_w_s[l], gm_b_s[l])
        o_c = hgrn_mixer(hg_q, hg_ff, hg_fb, hg_i, hg_g, lb_all[l], hg_g_o[l])
        m = jnp.concatenate([o_a, o_b, o_c], axis=-1) @ w_out[l]
        x = x + rms_norm(m, g_post_mix[l])
        h = rms_norm(x, g_pre_ffn[l])
        f = jnp.square(jax.nn.relu(h @ w_ff1[l])) @ w_ff2[l]
        x = x + rms_norm(f, g_post_ffn[l])
    return x


def setup_inputs(seed: int = 0) -> dict:
    key = jax.random.key(seed)
    ks = jax.random.split(key, 24)
    f32 = jnp.float32

    def nrm(k, shape, fan_in):
        return jax.random.normal(k, shape, f32) * (fan_in ** -0.5)

    def gain(k, shape):
        return 1.0 + 0.02 * jax.random.normal(k, shape, f32)

    return {
        'x_prompt': jax.random.normal(ks[0], (BATCH, SEQ, D_MODEL), f32),
        'x_sample': jax.random.normal(ks[1], (DEC_BATCH, DEC_SEQ, D_MODEL), f32),
        'hg_lb_logits': 0.5 * jax.random.normal(ks[2], (DEPTH, 2, HG_KEYS), f32),
        'g_pre_mix': gain(ks[3], (DEPTH, D_MODEL)),
        'w_in': nrm(ks[4], (DEPTH, D_MODEL, IN_COLS), D_MODEL),
        'g_q_a': gain(ks[5], (DEPTH, MLA_Q_RANK)),
        'w_uq': nrm(ks[6], (DEPTH, MLA_Q_RANK, MLA_HEADS * (MLA_NOPE + MLA_ROPE)), MLA_Q_RANK),
        'g_kv_a': gain(ks[7], (DEPTH, MLA_KV_RANK)),
        'w_ukv': nrm(ks[8], (DEPTH, MLA_KV_RANK, MLA_HEADS * (MLA_NOPE + MLA_V)), MLA_KV_RANK),
        'gm_ln_g': gain(ks[9], (DEPTH, GM_WIDTH)),
        'gm_ln_b': 0.02 * jax.random.normal(ks[10], (DEPTH, GM_WIDTH), f32),
        'gm_w_s': nrm(ks[11], (DEPTH, GM_CHUNK, GM_CHUNK), GM_CHUNK),
        'gm_b_s': 0.02 * jax.random.normal(ks[12], (DEPTH, GM_CHUNK), f32),
        'hg_g_o': gain(ks[13], (DEPTH, HG_WIDTH)),
        'w_out': nrm(ks[14], (DEPTH, MIX_WIDTH, D_MODEL), MIX_WIDTH),
        'g_post_mix': gain(ks[15], (DEPTH, D_MODEL)),
        'g_pre_ffn': gain(ks[16], (DEPTH, D_MODEL)),
        'w_ff1': nrm(ks[17], (DEPTH, D_MODEL, D_FF), D_MODEL),
        'w_ff2': nrm(ks[18], (DEPTH, D_FF, D_MODEL), D_FF),
        'g_post_ffn': gain(ks[19], (DEPTH, D_MODEL)),
    }


def reference(x_prompt, x_sample, hg_lb_logits, g_pre_mix, w_in, g_q_a, w_uq, g_kv_a, w_ukv, gm_ln_g, gm_ln_b,
              gm_w_s, gm_b_s, hg_g_o, w_out, g_post_mix, g_pre_ffn, w_ff1, w_ff2, g_post_ffn):
    y_prompt = encoder_trunk(x_prompt, hg_lb_logits, g_pre_mix, w_in, g_q_a, w_uq, g_kv_a, w_ukv, gm_ln_g, gm_ln_b,
                             gm_w_s, gm_b_s, hg_g_o, w_out, g_post_mix, g_pre_ffn, w_ff1, w_ff2, g_post_ffn)
    y_sample = encoder_trunk(x_sample, hg_lb_logits, g_pre_mix, w_in, g_q_a, w_uq, g_kv_a, w_ukv, gm_ln_g, gm_ln_b,
                             gm_w_s, gm_b_s, hg_g_o, w_out, g_post_mix, g_pre_ffn, w_ff1, w_ff2, g_post_ffn)
    return (y_prompt, y_sample)
```

```python
import functools

import numpy as np
import jax
import jax.numpy as jnp
from jax import lax
from jax.experimental import pallas as pl
from jax.experimental.pallas import tpu as pltpu

D_MODEL = 1024
MLA_HEADS = 8
MLA_NOPE = 64
MLA_ROPE = 32
MLA_V = 64
MLA_Q_RANK = D_MODEL // 4
MLA_KV_RANK = D_MODEL // 8
ROPE_BASE = 10000.0
GM_WIDTH = D_MODEL // 4
GM_GROUPS = 4
GM_CHUNK = 128
HG_HEADS = 4
HG_DK = 64
HG_KEYS = HG_HEADS * HG_DK
HG_WIDTH = HG_HEADS * 64
HG_CHUNK = 32
F_FLOOR = 1e-20
D_FF = 4 * D_MODEL
EPS = 1e-6

LANES = 128
HEAD_PAD = LANES
QK_WIDTH = MLA_HEADS * HEAD_PAD
V_WIDTH = MLA_HEADS * MLA_V
C_Q, C_KV, C_KPE, C_KPS, C_GU, C_GV, C_HQ, C_HF, C_HI, C_HG, C_END = (
    0, 256, 384, 512, 640, 896, 1152, 1408, 1920, 2176, 2432)
VMEM_LIMIT = 56 * 1024 * 1024
NEG_BIG = -1e30

F32 = jnp.float32
BF16 = jnp.bfloat16


def _rms(x, g):
    return x * lax.rsqrt(jnp.mean(x * x, axis=-1, keepdims=True) + EPS) * g


def _gelu(x):
    return 0.5 * x * (1.0 + lax.erf(x * (0.5 ** 0.5)))


def _seg_sum(x, ones_bd):
    hi = x.astype(BF16)
    lo = (x - hi.astype(F32)).astype(BF16)
    return (jnp.dot(hi, ones_bd, preferred_element_type=F32)
            + jnp.dot(lo, ones_bd, preferred_element_type=F32))


def _in_proj_kernel(x_ref, gpre_ref, win_ref, gq_ref, wq_ref, wqs_ref, gkv_ref, wk_ref, wv_ref,
                    ctab_ref, stab_ref, lng_ref, lnb_ref, ws_ref, bs_ref, lbl_ref, ones_ref,
                    q_out, k_out, v_out, ob_out, hqig_out, logf_out, kk_out, *, layer, tm):
    h = _rms(x_ref[...], gpre_ref[...]).astype(BF16)
    z = jnp.dot(h, win_ref[...], preferred_element_type=F32)
    ctab = ctab_ref[...]
    stab = stab_ref[...]
    scale = (MLA_NOPE + MLA_ROPE) ** -0.5

    cqn = _rms(z[:, C_Q:C_KV], gq_ref[...]).astype(BF16)
    q = jnp.dot(cqn, wq_ref[...], preferred_element_type=F32)
    qs = jnp.dot(cqn, wqs_ref[...], preferred_element_type=F32)
    for hd in range(MLA_HEADS):
        sl = slice(hd * HEAD_PAD, (hd + 1) * HEAD_PAD)
        q_out[:, sl] = ((q[:, sl] * ctab + qs[:, sl] * stab) * scale).astype(BF16)

    ckvn = _rms(z[:, C_KV:C_KPE], gkv_ref[...]).astype(BF16)
    kn = jnp.dot(ckvn, wk_ref[...], preferred_element_type=F32)
    kpe = z[:, C_KPE:C_KPS] * ctab + z[:, C_KPS:C_GU] * stab
    for hd in range(MLA_HEADS):
        sl = slice(hd * HEAD_PAD, (hd + 1) * HEAD_PAD)
        k_out[:, sl] = (kn[:, sl] + kpe).astype(BF16)
    v_out[...] = jnp.dot(ckvn, wv_ref[...], preferred_element_type=F32).astype(BF16)

    ones_bd = ones_ref[...]
    u = _gelu(z[:, C_GU:C_GV])
    v = _gelu(z[:, C_GV:C_HQ])
    inv_n = 1.0 / (GM_WIDTH // GM_GROUPS)
    xc = v - _seg_sum(v, ones_bd) * inv_n
    var = _seg_sum(xc * xc, ones_bd) * inv_n
    vn = (xc * lax.rsqrt(var + EPS) * lng_ref[...] + lnb_ref[...]).astype(BF16)
    ws = ws_ref[...]
    bs = bs_ref[...]
    for c in range(tm // GM_CHUNK):
        rs = slice(c * GM_CHUNK, (c + 1) * GM_CHUNK)
        vs = jnp.dot(ws, vn[rs], preferred_element_type=F32) + bs
        ob_out[rs, :] = (u[rs] * vs).astype(BF16)

    lg = lbl_ref[...]
    e = jnp.exp(lg - jnp.max(lg, axis=0, keepdims=True))
    p = e / jnp.sum(e, axis=0, keepdims=True)
    cum = p[0:1]
    for i in range(1, layer + 1):
        cum = cum + p[i:i + 1]
    lb = jnp.clip(cum - p[0:1], 0.0, 0.999)
    xf = z[:, C_HF:C_HI]
    f = lb + (1.0 - lb) * jax.nn.sigmoid(xf)
    logf_out[...] = jnp.log(jnp.maximum(f, F_FLOOR))
    kk_out[...] = (1.0 - lb) * jax.nn.sigmoid(-xf)
    hqig_out[:, 0:HG_KEYS] = z[:, C_HQ:C_HF].astype(BF16)
    hqig_out[:, HG_KEYS:] = z[:, C_HI:C_END].astype(BF16)


def _in_proj(x2d, wl, tabs, lbl, ones_bd, *, layer, seq, tm):
    T = x2d.shape[0]
    nt = T // tm
    npos = seq // tm
    const = lambda i: (0, 0)
    row = lambda i: (i, 0)
    pos = lambda i: (i % npos, 0)
    full = lambda a: pl.BlockSpec(a.shape, const)
    ins = [
        (x2d, pl.BlockSpec((tm, D_MODEL), row)),
        (wl['g_pre_mix'], None), (wl['w_in'], None), (wl['g_q_a'], None), (wl['w_q'], None),
        (wl['w_qs'], None), (wl['g_kv_a'], None), (wl['w_k'], None), (wl['w_v'], None),
        (tabs[0], pl.BlockSpec((tm, HEAD_PAD), pos)), (tabs[1], pl.BlockSpec((tm, HEAD_PAD), pos)),
        (wl['gm_ln_g'], None), (wl['gm_ln_b'], None), (wl['gm_w_s'], None), (wl['gm_b_s'], None),
        (lbl, None), (ones_bd, None),
    ]
    in_specs = [s if s is not None else full(a) for a, s in ins]
    outs = [(QK_WIDTH, BF16), (QK_WIDTH, BF16), (V_WIDTH, BF16), (GM_WIDTH, BF16),
            (3 * HG_KEYS, BF16), (2 * HG_KEYS, F32), (2 * HG_KEYS, F32)]
    return pl.pallas_call(
        functools.partial(_in_proj_kernel, layer=layer, tm=tm),
        grid=(nt,),
        in_specs=in_specs,
        out_specs=[pl.BlockSpec((tm, w), row) for w, _ in outs],
        out_shape=[jax.ShapeDtypeStruct((T, w), dt) for w, dt in outs],
        compiler_params=pltpu.CompilerParams(dimension_semantics=("arbitrary",),
                                             vmem_limit_bytes=VMEM_LIMIT),
        name="in_proj",
    )(*[a for a, _ in ins])


def _attn_kernel(q_ref, k_ref, v_ref, o_ref):
    v = v_ref[...]
    outs = []
    for hd in range(2):
        sl = slice(hd * HEAD_PAD, (hd + 1) * HEAD_PAD)
        s = lax.dot_general(q_ref[:, sl], k_ref[:, sl], (((1,), (1,)), ((), ())),
                            preferred_element_type=F32)
        p = jnp.exp(s - jnp.max(s, axis=-1, keepdims=True))
        l = jnp.sum(p, axis=-1, keepdims=True)
        o = jnp.dot(p.astype(BF16), v, preferred_element_type=F32)
        outs.append(o / l)
    lane = lax.broadcasted_iota(jnp.int32, outs[0].shape, 1)
    o_ref[...] = jnp.where(lane < MLA_V, outs[0], outs[1]).astype(BF16)


def _mla_attn(q, k, v, *, batch, seq, tq):
    T = q.shape[0]
    nq = seq // tq
    npair = MLA_HEADS // 2
    return pl.pallas_call(
        _attn_kernel,
        grid=(batch, npair, nq),
        in_specs=[
            pl.BlockSpec((tq, 2 * HEAD_PAD), lambda b, h, i: (b * nq + i, h)),
            pl.BlockSpec((seq, 2 * HEAD_PAD), lambda b, h, i: (b, h)),
            pl.BlockSpec((seq, 2 * MLA_V), lambda b, h, i: (b, h)),
        ],
        out_specs=pl.BlockSpec((tq, 2 * MLA_V), lambda b, h, i: (b * nq + i, h)),
        out_shape=jax.ShapeDtypeStruct((T, V_WIDTH), BF16),
        compiler_params=pltpu.CompilerParams(
            dimension_semantics=("arbitrary", "arbitrary", "arbitrary"),
            vmem_limit_bytes=VMEM_LIMIT),
        name="mla_attn",
    )(q, k, v)


SUB = 8
NSUB = HG_CHUNK // SUB
ROWS_PER_SUB = tuple(SUB * (NSUB - sb) for sb in range(NSUB))
LHS_ROWS = SUB * sum(ROWS_PER_SUB)


def _hgrn_dir(reverse, nchunk, q_ref, v_ref, lf_ref, kk_ref, tri_ref, ones_ref, bmask_ref,
              o_ref, st_ref, qf_ref, vf_ref, b_ref, lhs_ref, res_ref):
    qf_ref[...] = q_ref[...].astype(F32)
    vf_ref[...] = v_ref[...].astype(F32)
    b_ref[...] = jnp.dot(tri_ref[...], lf_ref[...], preferred_element_type=F32,
                         precision=lax.Precision.HIGHEST)
    ones_bd = ones_ref[...]
    bmask = bmask_ref[...]
    sub_iota = lax.broadcasted_iota(jnp.int32, (SUB, HG_KEYS), 0)

    def chunk_body(ci, carry):
        c = (nchunk - 1 - ci) if reverse else ci
        r0 = pl.multiple_of(c * HG_CHUNK, HG_CHUNK)
        rows = pl.ds(r0, HG_CHUNK)
        b = b_ref[rows, :]
        q = qf_ref[rows, :]
        kk = kk_ref[rows, :]
        v = vf_ref[rows, :]
        b_end = b[0:1] if reverse else b[HG_CHUNK - 1:HG_CHUNK]

        off = 0
        for sb in range(NSUB):
            if reverse:
                t0, t1 = 0, SUB * (sb + 1)
                diag = slice(t1 - SUB, t1)
            else:
                t0, t1 = SUB * sb, HG_CHUNK
                diag = slice(0, SUB)
            n = t1 - t0
            q_r = q[t0:t1]
            b_r = b[t0:t1]
            for si in range(SUB):
                s = SUB * sb + si
                d = b_r - b_ref[pl.ds(r0 + s, 1), :]
                keep = (sub_iota <= si) if reverse else (sub_iota >= si)
                d_diag = jnp.where(keep, d[diag], NEG_BIG)
                if reverse:
                    d = jnp.concatenate([d[:n - SUB], d_diag], axis=0) if n > SUB else d_diag
                else:
                    d = jnp.concatenate([d_diag, d[SUB:]], axis=0) if n > SUB else d_diag
                lhs_ref[off:off + n, :] = q_r * jnp.exp(d) * kk_ref[pl.ds(r0 + s, 1), :]
                off += n
        res_ref[...] = jnp.dot(lhs_ref[...].astype(BF16), ones_bd, preferred_element_type=F32)

        st = st_ref[...]
        qe = (q * jnp.exp(b)).astype(BF16)
        o_inter = lax.dot_general(qe, st.astype(BF16), (((1,), (1,)), ((), ())),
                                  preferred_element_type=F32)
        o_blk = [o_inter[SUB * tb:SUB * (tb + 1)] for tb in range(NSUB)]
        off = 0
        for sb in range(NSUB):
            t0 = 0 if reverse else SUB * sb
            n = SUB * (sb + 1) if reverse else HG_CHUNK - SUB * sb
            for si in range(SUB):
                s = SUB * sb + si
                contrib = res_ref[off:off + n, :] * vf_ref[pl.ds(r0 + s, 1), :]
                for j in range(n // SUB):
                    tb = t0 // SUB + j
                    o_blk[tb] = o_blk[tb] + contrib[SUB * j:SUB * (j + 1)]
                off += n
        o_ref[rows, :] = jnp.concatenate(o_blk, axis=0)

        kdec = (kk * jnp.exp(b_end - b)).astype(BF16)
        upd = lax.dot_general(v.astype(BF16), kdec, (((0,), (0,)), ((), ())),
                              preferred_element_type=F32)
        st_ref[...] = st * jnp.exp(b_end) + upd * bmask
        return carry

    lax.fori_loop(0, nchunk, chunk_body, 0)


def _hgrn_kernel(qf_in, vf_in, lff_in, kkf_in, qb_in, vb_in, lfb_in, kkb_in,
                 tril_ref, triu_ref, ones_ref, bmask_ref,
                 of_out, ob_out,
                 stf_ref, stb_ref, qf_ref, vf_ref, b_ref, lhs_ref, res_ref, *, nchunk):
    @pl.when(pl.program_id(1) == 0)
    def _():
        stf_ref[...] = jnp.zeros_like(stf_ref)
        stb_ref[...] = jnp.zeros_like(stb_ref)

    _hgrn_dir(False, nchunk, qf_in, vf_in, lff_in, kkf_in, tril_ref, ones_ref, bmask_ref,
              of_out, stf_ref, qf_ref, vf_ref, b_ref, lhs_ref, res_ref)
    _hgrn_dir(True, nchunk, qb_in, vb_in, lfb_in, kkb_in, triu_ref, ones_ref, bmask_ref,
              ob_out, stb_ref, qf_ref, vf_ref, b_ref, lhs_ref, res_ref)


def _hgrn_scan(hqig, logf, kk, consts, *, batch, seq, tb):
    T = hqig.shape[0]
    nb = seq // tb
    fwd = lambda col: (lambda b, j: (b * nb + j, col))
    bwd = lambda col: (lambda b, j: (b * nb + nb - 1 - j, col))
    blk = lambda m: pl.BlockSpec((tb, HG_KEYS), m)
    const = lambda a: pl.BlockSpec(a.shape, lambda b, j: (0, 0))
    tril, triu, ones_bd, bmask = consts
    scratch = [pltpu.VMEM((HG_WIDTH, HG_KEYS), F32), pltpu.VMEM((HG_WIDTH, HG_KEYS), F32),
               pltpu.VMEM((tb, HG_KEYS), F32), pltpu.VMEM((tb, HG_KEYS), F32),
               pltpu.VMEM((tb, HG_KEYS), F32),
               pltpu.VMEM((LHS_ROWS, HG_KEYS), F32), pltpu.VMEM((LHS_ROWS, HG_KEYS), F32)]
    return pl.pallas_call(
        functools.partial(_hgrn_kernel, nchunk=tb // HG_CHUNK),
        grid=(batch, nb),
        in_specs=[blk(fwd(0)), blk(fwd(1)), blk(fwd(0)), blk(fwd(0)),
                  blk(bwd(0)), blk(bwd(1)), blk(bwd(1)), blk(bwd(1)),
                  const(tril), const(triu), const(ones_bd), const(bmask)],
        out_specs=[blk(fwd(0)), blk(bwd(0))],
        out_shape=[jax.ShapeDtypeStruct((T, HG_WIDTH), F32)] * 2,
        scratch_shapes=scratch,
        compiler_params=pltpu.CompilerParams(dimension_semantics=("arbitrary", "arbitrary"),
                                             vmem_limit_bytes=VMEM_LIMIT),
        name="hgrn_scan",
    )(hqig, hqig, logf, kk, hqig, hqig, logf, kk, tril, triu, ones_bd, bmask)


FF_BLOCK = 1024


def _out_ffn_kernel(x_ref, oa_ref, ob_ref, of_ref, obw_ref, g_ref, go_ref, ones_ref,
                    wo_ref, gpm_ref, gpf_ref, w1_ref, w2_ref, gpo_ref, y_ref):
    o = of_ref[...] + obw_ref[...]
    ms = _seg_sum(o * o, ones_ref[...]) * (1.0 / 64)
    gate = g_ref[...].astype(F32)
    oc = o * lax.rsqrt(ms + EPS) * go_ref[...] * (gate * jax.nn.sigmoid(gate))
    a0, a1 = V_WIDTH, V_WIDTH + GM_WIDTH
    m = (jnp.dot(oa_ref[...], wo_ref[0:a0, :], preferred_element_type=F32)
         + jnp.dot(ob_ref[...], wo_ref[a0:a1, :], preferred_element_type=F32)
         + jnp.dot(oc.astype(BF16), wo_ref[a1:, :], preferred_element_type=F32))
    x1 = x_ref[...] + _rms(m, gpm_ref[...])
    h = _rms(x1, gpf_ref[...]).astype(BF16)
    acc = jnp.zeros(x1.shape, F32)
    for c in range(D_FF // FF_BLOCK):
        cs = slice(c * FF_BLOCK, (c + 1) * FF_BLOCK)
        a = jnp.maximum(jnp.dot(h, w1_ref[:, cs], preferred_element_type=F32), 0.0)
        acc = acc + jnp.dot((a * a).astype(BF16), w2_ref[cs, :], preferred_element_type=F32)
    y_ref[...] = x1 + _rms(acc, gpo_ref[...])


def _out_ffn(x2d, oa, ob, of, obw, hqig, wl, ones_bd, *, tm):
    T = x2d.shape[0]
    row = lambda i: (i, 0)
    const = lambda a: pl.BlockSpec(a.shape, lambda i: (0, 0), pipeline_mode=pl.Buffered(1))
    ws = [wl['hg_g_o'], ones_bd, wl['w_out'], wl['g_post_mix'], wl['g_pre_ffn'],
          wl['w_ff1'], wl['w_ff2'], wl['g_post_ffn']]
    return pl.pallas_call(
        _out_ffn_kernel,
        grid=(T // tm,),
        in_specs=[pl.BlockSpec((tm, D_MODEL), row), pl.BlockSpec((tm, V_WIDTH), row),
                  pl.BlockSpec((tm, GM_WIDTH), row), pl.BlockSpec((tm, HG_WIDTH), row),
                  pl.BlockSpec((tm, HG_WIDTH), row),
                  pl.BlockSpec((tm, HG_WIDTH), lambda i: (i, 2))] + [const(a) for a in ws],
        out_specs=pl.BlockSpec((tm, D_MODEL), row),
        out_shape=jax.ShapeDtypeStruct((T, D_MODEL), F32),
        compiler_params=pltpu.CompilerParams(dimension_semantics=("arbitrary",),
                                             vmem_limit_bytes=VMEM_LIMIT),
        name="out_ffn",
    )(x2d, oa, ob, of, obw, hqig, *ws)


def _rope_tables(seq):
    half = MLA_ROPE // 2
    inv = ROPE_BASE ** (-(jnp.arange(half, dtype=F32) / half))
    ang = jnp.arange(seq, dtype=F32)[:, None] * inv[None, :]
    cos, sin = jnp.cos(ang), jnp.sin(ang)
    pad = jnp.zeros((seq, HEAD_PAD - MLA_NOPE - MLA_ROPE), F32)
    ctab = jnp.concatenate([jnp.ones((seq, MLA_NOPE), F32), cos, cos, pad], axis=1)
    stab = jnp.concatenate([jnp.zeros((seq, MLA_NOPE), F32), -sin, sin, pad], axis=1)
    return ctab, stab


def _pack_weights(l, g_pre_mix, w_in, g_q_a, w_uq, g_kv_a, w_ukv, gm_ln_g, gm_ln_b, gm_w_s, gm_b_s,
                  hg_g_o, w_out, g_post_mix, g_pre_ffn, w_ff1, w_ff2, g_post_ffn):
    half = MLA_ROPE // 2
    swap = lambda a: jnp.concatenate([a[..., half:], a[..., :half]], axis=-1)
    row = lambda a: a[l][None, :]
    zpad = lambda n, k: jnp.zeros((n, k), F32)

    wi = w_in[l]
    o = np.cumsum((0, MLA_Q_RANK, MLA_KV_RANK, MLA_ROPE, GM_WIDTH, GM_WIDTH,
                   HG_KEYS, HG_KEYS, HG_KEYS, HG_WIDTH, HG_WIDTH))
    kr = wi[:, o[2]:o[3]]
    tail = HEAD_PAD - MLA_NOPE - MLA_ROPE
    kpe = jnp.concatenate([zpad(D_MODEL, MLA_NOPE), kr, zpad(D_MODEL, tail)], axis=1)
    kps = jnp.concatenate([zpad(D_MODEL, MLA_NOPE), swap(kr), zpad(D_MODEL, tail)], axis=1)
    w_in_p = jnp.concatenate([wi[:, o[0]:o[2]], kpe, kps, wi[:, o[3]:]], axis=1)

    wq = w_uq[l].reshape(MLA_Q_RANK, MLA_HEADS, MLA_NOPE + MLA_ROPE)
    qn, qr = wq[..., :MLA_NOPE], wq[..., MLA_NOPE:]
    zq = jnp.zeros((MLA_Q_RANK, MLA_HEADS, tail), F32)
    w_q = jnp.concatenate([qn, qr, zq], axis=-1).reshape(MLA_Q_RANK, QK_WIDTH)
    w_qs = jnp.concatenate([jnp.zeros_like(qn), swap(qr), zq], axis=-1).reshape(MLA_Q_RANK, QK_WIDTH)

    wkv = w_ukv[l].reshape(MLA_KV_RANK, MLA_HEADS, MLA_NOPE + MLA_V)
    kn, vv = wkv[..., :MLA_NOPE], wkv[..., MLA_NOPE:]
    zk = jnp.zeros((MLA_KV_RANK, MLA_HEADS, HEAD_PAD - MLA_NOPE), F32)
    w_k = jnp.concatenate([kn, zk], axis=-1).reshape(MLA_KV_RANK, QK_WIDTH)
    w_v = vv.reshape(MLA_KV_RANK, V_WIDTH)

    return dict(
        g_pre_mix=row(g_pre_mix), w_in=w_in_p.astype(BF16), g_q_a=row(g_q_a), w_q=w_q.astype(BF16),
        w_qs=w_qs.astype(BF16), g_kv_a=row(g_kv_a), w_k=w_k.astype(BF16), w_v=w_v.astype(BF16),
        gm_ln_g=row(gm_ln_g), gm_ln_b=row(gm_ln_b), gm_w_s=gm_w_s[l].astype(BF16),
        gm_b_s=jnp.broadcast_to(gm_b_s[l][:, None], (GM_CHUNK, GM_WIDTH)),
        hg_g_o=row(hg_g_o), w_out=w_out[l].astype(BF16), g_post_mix=row(g_post_mix),
        g_pre_ffn=row(g_pre_ffn), w_ff1=w_ff1[l].astype(BF16), w_ff2=w_ff2[l].astype(BF16),
        g_post_ffn=row(g_post_ffn))


def _scan_consts(tb):
    t = np.arange(tb)
    same = (t[:, None] // HG_CHUNK) == (t[None, :] // HG_CHUNK)
    tril = jnp.asarray(same & (t[None, :] <= t[:, None]), F32)
    triu = jnp.asarray(same & (t[None, :] >= t[:, None]), F32)
    k = np.arange(HG_KEYS)
    bd = (k[:, None] // HG_DK) == (k[None, :] // HG_DK)
    return tril, triu, jnp.asarray(bd, BF16), jnp.asarray(bd, F32)


def _trunk(x, layers, lbl, *, tm_in, tq, tb, tm_out):
    batch, seq, _ = x.shape
    x2d = x.reshape(batch * seq, D_MODEL)
    tabs = _rope_tables(seq)
    consts = _scan_consts(tb)
    ones_bd = consts[2]
    for l, wl in enumerate(layers):
        q, k, v, ob, hqig, logf, kk = _in_proj(x2d, wl, tabs, lbl, ones_bd, layer=l, seq=seq, tm=tm_in)
        oa = _mla_attn(q, k, v, batch=batch, seq=seq, tq=tq)
        of, obw = _hgrn_scan(hqig, logf, kk, consts, batch=batch, seq=seq, tb=tb)
        x2d = _out_ffn(x2d, oa, ob, of, obw, hqig, wl, ones_bd, tm=tm_out)
    return x2d.reshape(batch, seq, D_MODEL)


def kernel(x_prompt, x_sample, hg_lb_logits, g_pre_mix, w_in, g_q_a, w_uq, g_kv_a, w_ukv, gm_ln_g, gm_ln_b,
           gm_w_s, gm_b_s, hg_g_o, w_out, g_post_mix, g_pre_ffn, w_ff1, w_ff2, g_post_ffn):
    depth = w_in.shape[0]
    layers = [_pack_weights(l, g_pre_mix, w_in, g_q_a, w_uq, g_kv_a, w_ukv, gm_ln_g, gm_ln_b, gm_w_s,
                            gm_b_s, hg_g_o, w_out, g_post_mix, g_pre_ffn, w_ff1, w_ff2, g_post_ffn)
              for l in range(depth)]
    lbl = hg_lb_logits.reshape(depth, 2 * HG_KEYS)
    y_prompt = _trunk(x_prompt, layers, lbl, tm_in=512, tq=256, tb=256, tm_out=256)
    y_sample = _trunk(x_sample, layers, lbl, tm_in=512, tq=128, tb=256, tm_out=256)
    return (y_prompt, y_sample)
```

```python
import functools

import numpy as np
import jax
import jax.numpy as jnp
from jax import lax
from jax.experimental import pallas as pl
from jax.experimental.pallas import tpu as pltpu

D_MODEL = 1024
MLA_HEADS = 8
MLA_NOPE = 64
MLA_ROPE = 32
MLA_V = 64
MLA_Q_RANK = D_MODEL // 4
MLA_KV_RANK = D_MODEL // 8
ROPE_BASE = 10000.0
GM_WIDTH = D_MODEL // 4
GM_GROUPS = 4
GM_CHUNK = 128
HG_HEADS = 4
HG_DK = 64
HG_KEYS = HG_HEADS * HG_DK
HG_WIDTH = HG_HEADS * 64
HG_CHUNK = 32
F_FLOOR = 1e-20
D_FF = 4 * D_MODEL
EPS = 1e-6

LANES = 128
HEAD_PAD = LANES
QK_WIDTH = MLA_HEADS * HEAD_PAD
V_WIDTH = MLA_HEADS * MLA_V
C_Q, C_KV, C_KPE, C_KPS, C_GU, C_GV, C_HQ, C_HF, C_HI, C_HG, C_END = (
    0, 256, 384, 512, 640, 896, 1152, 1408, 1920, 2176, 2432)
VMEM_LIMIT = 56 * 1024 * 1024
NEG_BIG = -1e30
LOG2E = 1.4426950408889634

F32 = jnp.float32
BF16 = jnp.bfloat16


def _rms(x, g):
    return x * lax.rsqrt(jnp.mean(x * x, axis=-1, keepdims=True) + EPS) * g


def _gelu(x):
    return 0.5 * x * (1.0 + lax.erf(x * (0.5 ** 0.5)))


def _seg_sum(x, ones_bd):
    hi = x.astype(BF16)
    lo = (x - hi.astype(F32)).astype(BF16)
    return (jnp.dot(hi, ones_bd, preferred_element_type=F32)
            + jnp.dot(lo, ones_bd, preferred_element_type=F32))


def _in_proj_kernel(x_ref, gpre_ref, win_ref, gq_ref, wq_ref, wqs_ref, gkv_ref, wk_ref, wv_ref,
                    ctab_ref, stab_ref, lng_ref, lnb_ref, ws_ref, bs_ref, lbl_ref, ones_ref,
                    q_out, k_out, v_out, ob_out, hqig_out, logf_out, kk_out, *, layer, tm):
    h = _rms(x_ref[...], gpre_ref[...]).astype(BF16)
    z = jnp.dot(h, win_ref[...], preferred_element_type=F32)
    ctab = ctab_ref[...]
    stab = stab_ref[...]
    scale = (MLA_NOPE + MLA_ROPE) ** -0.5 * LOG2E

    cqn = _rms(z[:, C_Q:C_KV], gq_ref[...]).astype(BF16)
    q = jnp.dot(cqn, wq_ref[...], preferred_element_type=F32)
    qs = jnp.dot(cqn, wqs_ref[...], preferred_element_type=F32)
    for hd in range(MLA_HEADS):
        sl = slice(hd * HEAD_PAD, (hd + 1) * HEAD_PAD)
        q_out[:, sl] = ((q[:, sl] * ctab + qs[:, sl] * stab) * scale).astype(BF16)

    ckvn = _rms(z[:, C_KV:C_KPE], gkv_ref[...]).astype(BF16)
    kn = jnp.dot(ckvn, wk_ref[...], preferred_element_type=F32)
    kpe = z[:, C_KPE:C_KPS] * ctab + z[:, C_KPS:C_GU] * stab
    for hd in range(MLA_HEADS):
        sl = slice(hd * HEAD_PAD, (hd + 1) * HEAD_PAD)
        k_out[:, sl] = (kn[:, sl] + kpe).astype(BF16)
    v_out[...] = jnp.dot(ckvn, wv_ref[...], preferred_element_type=F32).astype(BF16)

    ones_bd = ones_ref[...]
    u = _gelu(z[:, C_GU:C_GV])
    v = _gelu(z[:, C_GV:C_HQ])
    inv_n = 1.0 / (GM_WIDTH // GM_GROUPS)
    xc = v - _seg_sum(v, ones_bd) * inv_n
    var = _seg_sum(xc * xc, ones_bd) * inv_n
    vn = (xc * lax.rsqrt(var + EPS) * lng_ref[...] + lnb_ref[...]).astype(BF16)
    ws = ws_ref[...]
    bs = bs_ref[...]
    for c in range(tm // GM_CHUNK):
        rs = slice(c * GM_CHUNK, (c + 1) * GM_CHUNK)
        vs = jnp.dot(ws, vn[rs], preferred_element_type=F32) + bs
        ob_out[rs, :] = (u[rs] * vs).astype(BF16)

    lg = lbl_ref[...]
    e = jnp.exp(lg - jnp.max(lg, axis=0, keepdims=True))
    p = e / jnp.sum(e, axis=0, keepdims=True)
    cum = p[0:1]
    for i in range(1, layer + 1):
        cum = cum + p[i:i + 1]
    lb = jnp.clip(cum - p[0:1], 0.0, 0.999)
    xf = z[:, C_HF:C_HI]
    f = lb + (1.0 - lb) * jax.nn.sigmoid(xf)
    logf_out[...] = jnp.log(jnp.maximum(f, F_FLOOR)) * LOG2E
    kk_out[...] = (1.0 - lb) * jax.nn.sigmoid(-xf)
    hqig_out[:, 0:HG_KEYS] = z[:, C_HQ:C_HF].astype(BF16)
    hqig_out[:, HG_KEYS:] = z[:, C_HI:C_END].astype(BF16)


def _in_proj(x2d, wl, tabs, lbl, ones_bd, *, layer, seq, tm):
    T = x2d.shape[0]
    nt = T // tm
    npos = seq // tm
    const = lambda i: (0, 0)
    row = lambda i: (i, 0)
    pos = lambda i: (i % npos, 0)
    full = lambda a: pl.BlockSpec(a.shape, const)
    ins = [
        (x2d, pl.BlockSpec((tm, D_MODEL), row)),
        (wl['g_pre_mix'], None), (wl['w_in'], None), (wl['g_q_a'], None), (wl['w_q'], None),
        (wl['w_qs'], None), (wl['g_kv_a'], None), (wl['w_k'], None), (wl['w_v'], None),
        (tabs[0], pl.BlockSpec((tm, HEAD_PAD), pos)), (tabs[1], pl.BlockSpec((tm, HEAD_PAD), pos)),
        (wl['gm_ln_g'], None), (wl['gm_ln_b'], None), (wl['gm_w_s'], None), (wl['gm_b_s'], None),
        (lbl, None), (ones_bd, None),
    ]
    in_specs = [s if s is not None else full(a) for a, s in ins]
    outs = [(QK_WIDTH, BF16), (QK_WIDTH, BF16), (V_WIDTH, BF16), (GM_WIDTH, BF16),
            (3 * HG_KEYS, BF16), (2 * HG_KEYS, F32), (2 * HG_KEYS, F32)]
    return pl.pallas_call(
        functools.partial(_in_proj_kernel, layer=layer, tm=tm),
        grid=(nt,),
        in_specs=in_specs,
        out_specs=[pl.BlockSpec((tm, w), row) for w, _ in outs],
        out_shape=[jax.ShapeDtypeStruct((T, w), dt) for w, dt in outs],
        compiler_params=pltpu.CompilerParams(dimension_semantics=("arbitrary",),
                                             vmem_limit_bytes=VMEM_LIMIT),
        name="in_proj",
    )(*[a for a, _ in ins])


def _attn_kernel(q_ref, k_ref, v_ref, o_ref, s_ref, *, ck, nh):
    tq, seq = q_ref.shape[0], k_ref.shape[0]
    nck = seq // ck
    outs = []
    for hd in range(nh):
        sl = slice(hd * HEAD_PAD, (hd + 1) * HEAD_PAD)
        vsl = slice((hd // 2) * 2 * MLA_V, (hd // 2 + 1) * 2 * MLA_V)
        buf = hd % 2
        q = q_ref[:, sl]
        mp = None
        for c in range(nck):
            cs = slice(c * ck, (c + 1) * ck)
            s = lax.dot_general(q, k_ref[cs, sl], (((1,), (1,)), ((), ())),
                                preferred_element_type=F32)
            s_ref[buf, :, cs] = s
            for j in range(ck // LANES):
                t = s[:, j * LANES:(j + 1) * LANES]
                mp = t if mp is None else jnp.maximum(mp, t)
        m = jnp.max(mp, axis=-1, keepdims=True)
        lp = jnp.zeros((tq, LANES), F32)
        acc = jnp.zeros((tq, 2 * MLA_V), F32)
        for c in range(nck):
            cs = slice(c * ck, (c + 1) * ck)
            p = jnp.exp2(s_ref[buf, :, cs] - m)
            for j in range(ck // LANES):
                lp = lp + p[:, j * LANES:(j + 1) * LANES]
            acc = acc + jnp.dot(p.astype(BF16), v_ref[cs, vsl], preferred_element_type=F32)
        outs.append(acc / jnp.sum(lp, axis=-1, keepdims=True))
    lane = lax.broadcasted_iota(jnp.int32, outs[0].shape, 1)
    for pr in range(nh // 2):
        o_ref[:, pr * 2 * MLA_V:(pr + 1) * 2 * MLA_V] = jnp.where(
            lane < MLA_V, outs[2 * pr], outs[2 * pr + 1]).astype(BF16)


def _mla_attn(q, k, v, *, batch, seq, tq, ck, nh):
    T = q.shape[0]
    nq = seq // tq
    return pl.pallas_call(
        functools.partial(_attn_kernel, ck=ck, nh=nh),
        grid=(batch, MLA_HEADS // nh, nq),
        in_specs=[
            pl.BlockSpec((tq, nh * HEAD_PAD), lambda b, h, i: (b * nq + i, h)),
            pl.BlockSpec((seq, nh * HEAD_PAD), lambda b, h, i: (b, h)),
            pl.BlockSpec((seq, nh * MLA_V), lambda b, h, i: (b, h)),
        ],
        out_specs=pl.BlockSpec((tq, nh * MLA_V), lambda b, h, i: (b * nq + i, h)),
        out_shape=jax.ShapeDtypeStruct((T, V_WIDTH), BF16),
        scratch_shapes=[pltpu.VMEM((2, tq, seq), F32)],
        compiler_params=pltpu.CompilerParams(
            dimension_semantics=("arbitrary", "arbitrary", "arbitrary"),
            vmem_limit_bytes=VMEM_LIMIT),
        name="mla_attn",
    )(q, k, v)


SUB = 8
NSUB = HG_CHUNK // SUB
ROWS_PER_SUB = tuple(SUB * (NSUB - sb) for sb in range(NSUB))
LHS_ROWS = SUB * sum(ROWS_PER_SUB)


def _chunk_cumsum(x, reverse):
    n = x.shape[0]
    pos = lax.broadcasted_iota(jnp.int32, x.shape, 0) & (HG_CHUNK - 1)
    d = 1
    while d < HG_CHUNK:
        if reverse:
            shifted, ok = pltpu.roll(x, n - d, 0), pos < HG_CHUNK - d
        else:
            shifted, ok = pltpu.roll(x, d, 0), pos >= d
        x = x + jnp.where(ok, shifted, 0.0)
        d *= 2
    return x


def _hgrn_chunk(reverse, c, kk_ref, ones_ref, bmask_ref,
                o_ref, st_ref, qf_ref, vf_ref, b_ref, lhs_ref, res_ref):
    ones_bd = ones_ref[...]
    bmask = bmask_ref[...]
    sub_iota = lax.broadcasted_iota(jnp.int32, (SUB, HG_KEYS), 0)
    r0 = pl.multiple_of(c * HG_CHUNK, HG_CHUNK)
    rows = pl.ds(r0, HG_CHUNK)
    b = b_ref[rows, :]
    q = qf_ref[rows, :]
    kk = kk_ref[rows, :]
    v = vf_ref[rows, :]
    b_end = b[0:1] if reverse else b[HG_CHUNK - 1:HG_CHUNK]

    off = 0
    for sb in range(NSUB):
        if reverse:
            t0, t1 = 0, SUB * (sb + 1)
            diag = slice(t1 - SUB, t1)
        else:
            t0, t1 = SUB * sb, HG_CHUNK
            diag = slice(0, SUB)
        n = t1 - t0
        q_r = q[t0:t1]
        b_r = b[t0:t1]
        for si in range(SUB):
            s = SUB * sb + si
            d = b_r - b[s:s + 1]
            keep = (sub_iota <= si) if reverse else (sub_iota >= si)
            d_diag = jnp.where(keep, d[diag], NEG_BIG)
            if reverse:
                d = jnp.concatenate([d[:n - SUB], d_diag], axis=0) if n > SUB else d_diag
            else:
                d = jnp.concatenate([d_diag, d[SUB:]], axis=0) if n > SUB else d_diag
            lhs_ref[off:off + n, :] = q_r * jnp.exp2(d) * kk[s:s + 1]
            off += n
    res_ref[...] = jnp.dot(lhs_ref[...].astype(BF16), ones_bd, preferred_element_type=F32)

    st = st_ref[...]
    qe = (q * jnp.exp2(b)).astype(BF16)
    o_inter = lax.dot_general(qe, st.astype(BF16), (((1,), (1,)), ((), ())),
                              preferred_element_type=F32)
    o_blk = [o_inter[SUB * tb:SUB * (tb + 1)] for tb in range(NSUB)]
    off = 0
    for sb in range(NSUB):
        t0 = 0 if reverse else SUB * sb
        n = SUB * (sb + 1) if reverse else HG_CHUNK - SUB * sb
        for si in range(SUB):
            s = SUB * sb + si
            contrib = res_ref[off:off + n, :] * v[s:s + 1]
            for j in range(n // SUB):
                tb = t0 // SUB + j
                o_blk[tb] = o_blk[tb] + contrib[SUB * j:SUB * (j + 1)]
            off += n
    o_ref[rows, :] = jnp.concatenate(o_blk, axis=0)

    kdec = (kk * jnp.exp2(b_end - b)).astype(BF16)
    upd = lax.dot_general(v.astype(BF16), kdec, (((0,), (0,)), ((), ())),
                          preferred_element_type=F32)
    st_ref[...] = st * jnp.exp2(b_end) + upd * bmask


def _hgrn_kernel(qf_in, vf_in, lff_in, kkf_in, qb_in, vb_in, lfb_in, kkb_in,
                 ones_ref, bmask_ref,
                 of_out, ob_out,
                 st_ref, qf_ref, vf_ref, b_ref, lhs_ref, res_ref, *, nchunk):
    @pl.when(pl.program_id(1) == 0)
    def _():
        st_ref[...] = jnp.zeros_like(st_ref)

    for d, (q_in, v_in, lf_in) in enumerate(((qf_in, vf_in, lff_in), (qb_in, vb_in, lfb_in))):
        qf_ref[d] = q_in[...].astype(F32)
        vf_ref[d] = v_in[...].astype(F32)
        b_ref[d] = _chunk_cumsum(lf_in[...], reverse=bool(d))

    def body(ci, carry):
        _hgrn_chunk(False, ci, kkf_in, ones_ref, bmask_ref, of_out, st_ref.at[0],
                    qf_ref.at[0], vf_ref.at[0], b_ref.at[0], lhs_ref.at[0], res_ref.at[0])
        _hgrn_chunk(True, nchunk - 1 - ci, kkb_in, ones_ref, bmask_ref, ob_out, st_ref.at[1],
                    qf_ref.at[1], vf_ref.at[1], b_ref.at[1], lhs_ref.at[1], res_ref.at[1])
        return carry

    lax.fori_loop(0, nchunk, body, 0)


def _hgrn_scan(hqig, logf, kk, consts, *, batch, seq, tb):
    T = hqig.shape[0]
    nb = seq // tb
    fwd = lambda col: (lambda b, j: (b * nb + j, col))
    bwd = lambda col: (lambda b, j: (b * nb + nb - 1 - j, col))
    blk = lambda m: pl.BlockSpec((tb, HG_KEYS), m)
    const = lambda a: pl.BlockSpec(a.shape, lambda b, j: (0, 0))
    ones_bd, bmask = consts
    scratch = [pltpu.VMEM((2, HG_WIDTH, HG_KEYS), F32),
               pltpu.VMEM((2, tb, HG_KEYS), F32), pltpu.VMEM((2, tb, HG_KEYS), F32),
               pltpu.VMEM((2, tb, HG_KEYS), F32),
               pltpu.VMEM((2, LHS_ROWS, HG_KEYS), F32), pltpu.VMEM((2, LHS_ROWS, HG_KEYS), F32)]
    return pl.pallas_call(
        functools.partial(_hgrn_kernel, nchunk=tb // HG_CHUNK),
        grid=(batch, nb),
        in_specs=[blk(fwd(0)), blk(fwd(1)), blk(fwd(0)), blk(fwd(0)),
                  blk(bwd(0)), blk(bwd(1)), blk(bwd(1)), blk(bwd(1)),
                  const(ones_bd), const(bmask)],
        out_specs=[blk(fwd(0)), blk(bwd(0))],
        out_shape=[jax.ShapeDtypeStruct((T, HG_WIDTH), F32)] * 2,
        scratch_shapes=scratch,
        compiler_params=pltpu.CompilerParams(dimension_semantics=("arbitrary", "arbitrary"),
                                             vmem_limit_bytes=VMEM_LIMIT),
        name="hgrn_scan",
    )(hqig, hqig, logf, kk, hqig, hqig, logf, kk, ones_bd, bmask)


FF_BLOCK = 1024


def _out_ffn_kernel(x_ref, oa_ref, ob_ref, of_ref, obw_ref, g_ref, go_ref, ones_ref,
                    wo_ref, gpm_ref, gpf_ref, w1_ref, w2_ref, gpo_ref, y_ref):
    o = of_ref[...] + obw_ref[...]
    ms = _seg_sum(o * o, ones_ref[...]) * (1.0 / 64)
    gate = g_ref[...].astype(F32)
    oc = o * lax.rsqrt(ms + EPS) * go_ref[...] * (gate * jax.nn.sigmoid(gate))
    a0, a1 = V_WIDTH, V_WIDTH + GM_WIDTH
    m = (jnp.dot(oa_ref[...], wo_ref[0:a0, :], preferred_element_type=F32)
         + jnp.dot(ob_ref[...], wo_ref[a0:a1, :], preferred_element_type=F32)
         + jnp.dot(oc.astype(BF16), wo_ref[a1:, :], preferred_element_type=F32))
    x1 = x_ref[...] + _rms(m, gpm_ref[...])
    h = _rms(x1, gpf_ref[...]).astype(BF16)
    acc = jnp.zeros(x1.shape, F32)
    for c in range(D_FF // FF_BLOCK):
        cs = slice(c * FF_BLOCK, (c + 1) * FF_BLOCK)
        a = jnp.maximum(jnp.dot(h, w1_ref[:, cs], preferred_element_type=F32), 0.0)
        acc = acc + jnp.dot((a * a).astype(BF16), w2_ref[cs, :], preferred_element_type=F32)
    y_ref[...] = x1 + _rms(acc, gpo_ref[...])


def _out_ffn(x2d, oa, ob, of, obw, hqig, wl, ones_bd, *, tm):
    T = x2d.shape[0]
    row = lambda i: (i, 0)
    const = lambda a: pl.BlockSpec(a.shape, lambda i: (0, 0), pipeline_mode=pl.Buffered(1))
    ws = [wl['hg_g_o'], ones_bd, wl['w_out'], wl['g_post_mix'], wl['g_pre_ffn'],
          wl['w_ff1'], wl['w_ff2'], wl['g_post_ffn']]
    return pl.pallas_call(
        _out_ffn_kernel,
        grid=(T // tm,),
        in_specs=[pl.BlockSpec((tm, D_MODEL), row), pl.BlockSpec((tm, V_WIDTH), row),
                  pl.BlockSpec((tm, GM_WIDTH), row), pl.BlockSpec((tm, HG_WIDTH), row),
                  pl.BlockSpec((tm, HG_WIDTH), row),
                  pl.BlockSpec((tm, HG_WIDTH), lambda i: (i, 2))] + [const(a) for a in ws],
        out_specs=pl.BlockSpec((tm, D_MODEL), row),
        out_shape=jax.ShapeDtypeStruct((T, D_MODEL), F32),
        compiler_params=pltpu.CompilerParams(dimension_semantics=("arbitrary",),
                                             vmem_limit_bytes=VMEM_LIMIT),
        name="out_ffn",
    )(x2d, oa, ob, of, obw, hqig, *ws)


def _rope_tables(seq):
    half = MLA_ROPE // 2
    inv = ROPE_BASE ** (-(jnp.arange(half, dtype=F32) / half))
    ang = jnp.arange(seq, dtype=F32)[:, None] * inv[None, :]
    cos, sin = jnp.cos(ang), jnp.sin(ang)
    pad = jnp.zeros((seq, HEAD_PAD - MLA_NOPE - MLA_ROPE), F32)
    ctab = jnp.concatenate([jnp.ones((seq, MLA_NOPE), F32), cos, cos, pad], axis=1)
    stab = jnp.concatenate([jnp.zeros((seq, MLA_NOPE), F32), -sin, sin, pad], axis=1)
    return ctab, stab


def _pack_weights(l, g_pre_mix, w_in, g_q_a, w_uq, g_kv_a, w_ukv, gm_ln_g, gm_ln_b, gm_w_s, gm_b_s,
                  hg_g_o, w_out, g_post_mix, g_pre_ffn, w_ff1, w_ff2, g_post_ffn):
    half = MLA_ROPE // 2
    swap = lambda a: jnp.concatenate([a[..., half:], a[..., :half]], axis=-1)
    row = lambda a: a[l][None, :]
    zpad = lambda n, k: jnp.zeros((n, k), F32)

    wi = w_in[l]
    o = np.cumsum((0, MLA_Q_RANK, MLA_KV_RANK, MLA_ROPE, GM_WIDTH, GM_WIDTH,
                   HG_KEYS, HG_KEYS, HG_KEYS, HG_WIDTH, HG_WIDTH))
    kr = wi[:, o[2]:o[3]]
    tail = HEAD_PAD - MLA_NOPE - MLA_ROPE
    kpe = jnp.concatenate([zpad(D_MODEL, MLA_NOPE), kr, zpad(D_MODEL, tail)], axis=1)
    kps = jnp.concatenate([zpad(D_MODEL, MLA_NOPE), swap(kr), zpad(D_MODEL, tail)], axis=1)
    w_in_p = jnp.concatenate([wi[:, o[0]:o[2]], kpe, kps, wi[:, o[3]:]], axis=1)

    wq = w_uq[l].reshape(MLA_Q_RANK, MLA_HEADS, MLA_NOPE + MLA_ROPE)
    qn, qr = wq[..., :MLA_NOPE], wq[..., MLA_NOPE:]
    zq = jnp.zeros((MLA_Q_RANK, MLA_HEADS, tail), F32)
    w_q = jnp.concatenate([qn, qr, zq], axis=-1).reshape(MLA_Q_RANK, QK_WIDTH)
    w_qs = jnp.concatenate([jnp.zeros_like(qn), swap(qr), zq], axis=-1).reshape(MLA_Q_RANK, QK_WIDTH)

    wkv = w_ukv[l].reshape(MLA_KV_RANK, MLA_HEADS, MLA_NOPE + MLA_V)
    kn, vv = wkv[..., :MLA_NOPE], wkv[..., MLA_NOPE:]
    zk = jnp.zeros((MLA_KV_RANK, MLA_HEADS, HEAD_PAD - MLA_NOPE), F32)
    w_k = jnp.concatenate([kn, zk], axis=-1).reshape(MLA_KV_RANK, QK_WIDTH)
    w_v = vv.reshape(MLA_KV_RANK, V_WIDTH)

    return dict(
        g_pre_mix=row(g_pre_mix), w_in=w_in_p.astype(BF16), g_q_a=row(g_q_a), w_q=w_q.astype(BF16),
        w_qs=w_qs.astype(BF16), g_kv_a=row(g_kv_a), w_k=w_k.astype(BF16), w_v=w_v.astype(BF16),
        gm_ln_g=row(gm_ln_g), gm_ln_b=row(gm_ln_b), gm_w_s=gm_w_s[l].astype(BF16),
        gm_b_s=jnp.broadcast_to(gm_b_s[l][:, None], (GM_CHUNK, GM_WIDTH)),
        hg_g_o=row(hg_g_o), w_out=w_out[l].astype(BF16), g_post_mix=row(g_post_mix),
        g_pre_ffn=row(g_pre_ffn), w_ff1=w_ff1[l].astype(BF16), w_ff2=w_ff2[l].astype(BF16),
        g_post_ffn=row(g_post_ffn))


def _scan_consts():
    k = np.arange(HG_KEYS)
    bd = (k[:, None] // HG_DK) == (k[None, :] // HG_DK)
    return jnp.asarray(bd, BF16), jnp.asarray(bd, F32)


def _trunk(x, layers, lbl, *, tm_in, tq, ck, nh, tb, tm_out):
    batch, seq, _ = x.shape
    x2d = x.reshape(batch * seq, D_MODEL)
    tabs = _rope_tables(seq)
    consts = _scan_consts()
    ones_bd = consts[0]
    for l, wl in enumerate(layers):
        q, k, v, ob, hqig, logf, kk = _in_proj(x2d, wl, tabs, lbl, ones_bd, layer=l, seq=seq, tm=tm_in)
        oa = _mla_attn(q, k, v, batch=batch, seq=seq, tq=tq, ck=ck, nh=nh)
        of, obw = _hgrn_scan(hqig, logf, kk, consts, batch=batch, seq=seq, tb=tb)
        x2d = _out_ffn(x2d, oa, ob, of, obw, hqig, wl, ones_bd, tm=tm_out)
    return x2d.reshape(batch, seq, D_MODEL)


def kernel(x_prompt, x_sample, hg_lb_logits, g_pre_mix, w_in, g_q_a, w_uq, g_kv_a, w_ukv, gm_ln_g, gm_ln_b,
           gm_w_s, gm_b_s, hg_g_o, w_out, g_post_mix, g_pre_ffn, w_ff1, w_ff2, g_post_ffn):
    depth = w_in.shape[0]
    layers = [_pack_weights(l, g_pre_mix, w_in, g_q_a, w_uq, g_kv_a, w_ukv, gm_ln_g, gm_ln_b, gm_w_s,
                            gm_b_s, hg_g_o, w_out, g_post_mix, g_pre_ffn, w_ff1, w_ff2, g_post_ffn)
              for l in range(depth)]
    lbl = hg_lb_logits.reshape(depth, 2 * HG_KEYS)
    y_prompt = _trunk(x_prompt, layers, lbl, tm_in=512, tq=512, ck=256, nh=8, tb=256, tm_out=512)
    y_sample = _trunk(x_sample, layers, lbl, tm_in=512, tq=256, ck=512, nh=4, tb=256, tm_out=512)
    return (y_prompt, y_sample)
```

```python
import functools

import numpy as np
import jax
import jax.numpy as jnp
from jax import lax
from jax.experimental import pallas as pl
from jax.experimental.pallas import tpu as pltpu

D_MODEL = 1024
MLA_HEADS = 8
MLA_NOPE = 64
MLA_ROPE = 32
MLA_V = 64
MLA_Q_RANK = D_MODEL // 4
MLA_KV_RANK = D_MODEL // 8
ROPE_BASE = 10000.0
GM_WIDTH = D_MODEL // 4
GM_GROUPS = 4
GM_CHUNK = 128
HG_HEADS = 4
HG_DK = 64
HG_KEYS = HG_HEADS * HG_DK
HG_WIDTH = HG_HEADS * 64
HG_CHUNK = 32
F_FLOOR = 1e-20
D_FF = 4 * D_MODEL
EPS = 1e-6

LANES = 128
HEAD_PAD = LANES
QK_WIDTH = MLA_HEADS * HEAD_PAD
V_WIDTH = MLA_HEADS * MLA_V
C_Q, C_KV, C_KPE, C_KPS, C_GU, C_GV, C_HQ, C_HF, C_HI, C_HG, C_END = (
    0, 256, 384, 512, 640, 896, 1152, 1408, 1920, 2176, 2432)
VMEM_LIMIT = 56 * 1024 * 1024
NEG_BIG = -1e30
LOG2E = 1.4426950408889634

F32 = jnp.float32
BF16 = jnp.bfloat16


def _rms(x, g):
    return x * lax.rsqrt(jnp.mean(x * x, axis=-1, keepdims=True) + EPS) * g


def _gelu(x):
    return 0.5 * x * (1.0 + lax.erf(x * (0.5 ** 0.5)))


def _seg_sum(x, ones_bd):
    hi = x.astype(BF16)
    lo = (x - hi.astype(F32)).astype(BF16)
    return (jnp.dot(hi, ones_bd, preferred_element_type=F32)
            + jnp.dot(lo, ones_bd, preferred_element_type=F32))


def _in_proj_kernel(x_ref, gpre_ref, win_ref, gq_ref, wq_ref, wqs_ref, gkv_ref, wk_ref, wv_ref,
                    ctab_ref, stab_ref, lng_ref, lnb_ref, ws_ref, bs_ref, lbl_ref, ones_ref,
                    q_out, k_out, v_out, ob_out, hqig_out, logf_out, kk_out, *, layer, tm):
    h = _rms(x_ref[...], gpre_ref[...]).astype(BF16)
    z = jnp.dot(h, win_ref[...], preferred_element_type=F32)
    ctab = ctab_ref[...]
    stab = stab_ref[...]
    scale = (MLA_NOPE + MLA_ROPE) ** -0.5 * LOG2E

    cqn = _rms(z[:, C_Q:C_KV], gq_ref[...]).astype(BF16)
    q = jnp.dot(cqn, wq_ref[...], preferred_element_type=F32)
    qs = jnp.dot(cqn, wqs_ref[...], preferred_element_type=F32)
    for hd in range(MLA_HEADS):
        sl = slice(hd * HEAD_PAD, (hd + 1) * HEAD_PAD)
        q_out[:, sl] = ((q[:, sl] * ctab + qs[:, sl] * stab) * scale).astype(BF16)

    ckvn = _rms(z[:, C_KV:C_KPE], gkv_ref[...]).astype(BF16)
    kn = jnp.dot(ckvn, wk_ref[...], preferred_element_type=F32)
    kpe = z[:, C_KPE:C_KPS] * ctab + z[:, C_KPS:C_GU] * stab
    for hd in range(MLA_HEADS):
        sl = slice(hd * HEAD_PAD, (hd + 1) * HEAD_PAD)
        k_out[:, sl] = (kn[:, sl] + kpe).astype(BF16)
    v_out[...] = jnp.dot(ckvn, wv_ref[...], preferred_element_type=F32).astype(BF16)

    ones_bd = ones_ref[...]
    u = _gelu(z[:, C_GU:C_GV])
    v = _gelu(z[:, C_GV:C_HQ])
    inv_n = 1.0 / (GM_WIDTH // GM_GROUPS)
    xc = v - _seg_sum(v, ones_bd) * inv_n
    var = _seg_sum(xc * xc, ones_bd) * inv_n
    vn = (xc * lax.rsqrt(var + EPS) * lng_ref[...] + lnb_ref[...]).astype(BF16)
    ws = ws_ref[...]
    bs = bs_ref[...]
    for c in range(tm // GM_CHUNK):
        rs = slice(c * GM_CHUNK, (c + 1) * GM_CHUNK)
        vs = jnp.dot(ws, vn[rs], preferred_element_type=F32) + bs
        ob_out[rs, :] = (u[rs] * vs).astype(BF16)

    lg = lbl_ref[...]
    e = jnp.exp(lg - jnp.max(lg, axis=0, keepdims=True))
    p = e / jnp.sum(e, axis=0, keepdims=True)
    cum = p[0:1]
    for i in range(1, layer + 1):
        cum = cum + p[i:i + 1]
    lb = jnp.clip(cum - p[0:1], 0.0, 0.999)
    xf = z[:, C_HF:C_HI]
    f = lb + (1.0 - lb) * jax.nn.sigmoid(xf)
    logf_out[...] = jnp.log(jnp.maximum(f, F_FLOOR)) * LOG2E
    kk_out[...] = (1.0 - lb) * jax.nn.sigmoid(-xf)
    hqig_out[:, 0:HG_KEYS] = z[:, C_HQ:C_HF].astype(BF16)
    hqig_out[:, HG_KEYS:] = z[:, C_HI:C_END].astype(BF16)


def _in_proj(x2d, wl, tabs, lbl, ones_bd, *, layer, seq, tm):
    T = x2d.shape[0]
    nt = T // tm
    npos = seq // tm
    const = lambda i: (0, 0)
    row = lambda i: (i, 0)
    pos = lambda i: (i % npos, 0)
    full = lambda a: pl.BlockSpec(a.shape, const)
    ins = [
        (x2d, pl.BlockSpec((tm, D_MODEL), row)),
        (wl['g_pre_mix'], None), (wl['w_in'], None), (wl['g_q_a'], None), (wl['w_q'], None),
        (wl['w_qs'], None), (wl['g_kv_a'], None), (wl['w_k'], None), (wl['w_v'], None),
        (tabs[0], pl.BlockSpec((tm, HEAD_PAD), pos)), (tabs[1], pl.BlockSpec((tm, HEAD_PAD), pos)),
        (wl['gm_ln_g'], None), (wl['gm_ln_b'], None), (wl['gm_w_s'], None), (wl['gm_b_s'], None),
        (lbl, None), (ones_bd, None),
    ]
    in_specs = [s if s is not None else full(a) for a, s in ins]
    outs = [(QK_WIDTH, BF16), (QK_WIDTH, BF16), (V_WIDTH, BF16), (GM_WIDTH, BF16),
            (3 * HG_KEYS, BF16), (2 * HG_KEYS, F32), (2 * HG_KEYS, F32)]
    return pl.pallas_call(
        functools.partial(_in_proj_kernel, layer=layer, tm=tm),
        grid=(nt,),
        in_specs=in_specs,
        out_specs=[pl.BlockSpec((tm, w), row) for w, _ in outs],
        out_shape=[jax.ShapeDtypeStruct((T, w), dt) for w, dt in outs],
        compiler_params=pltpu.CompilerParams(dimension_semantics=("arbitrary",),
                                             vmem_limit_bytes=VMEM_LIMIT),
        name="in_proj",
    )(*[a for a, _ in ins])


def _attn_head(hd, q_ref, k_ref, v_ref, s_ref, ck, outs):
    tq, seq = q_ref.shape[0], k_ref.shape[0]
    nck = seq // ck
    sl = slice(hd * HEAD_PAD, (hd + 1) * HEAD_PAD)
    vsl = slice((hd // 2) * 2 * MLA_V, (hd // 2 + 1) * 2 * MLA_V)
    buf = hd % 2
    q = q_ref[:, sl]
    mp = None
    for c in range(nck):
        cs = slice(c * ck, (c + 1) * ck)
        s = lax.dot_general(q, k_ref[cs, sl], (((1,), (1,)), ((), ())),
                            preferred_element_type=F32)
        s_ref[buf, :, cs] = s
        for j in range(ck // LANES):
            t = s[:, j * LANES:(j + 1) * LANES]
            mp = t if mp is None else jnp.maximum(mp, t)
        yield
    m = jnp.max(mp, axis=-1, keepdims=True)
    lp = jnp.zeros((tq, LANES), F32)
    acc = jnp.zeros((tq, 2 * MLA_V), F32)
    for c in range(nck):
        cs = slice(c * ck, (c + 1) * ck)
        p = jnp.exp2(s_ref[buf, :, cs] - m)
        for j in range(ck // LANES):
            lp = lp + p[:, j * LANES:(j + 1) * LANES]
        acc = acc + jnp.dot(p.astype(BF16), v_ref[cs, vsl], preferred_element_type=F32)
        yield
    outs.append(acc / jnp.sum(lp, axis=-1, keepdims=True))


SUB = 8
NSUB = HG_CHUNK // SUB
ROWS_PER_SUB = tuple(SUB * (NSUB - sb) for sb in range(NSUB))
LHS_ROWS = SUB * sum(ROWS_PER_SUB)


def _chunk_cumsum(x, reverse):
    n = x.shape[0]
    pos = lax.broadcasted_iota(jnp.int32, x.shape, 0) & (HG_CHUNK - 1)
    d = 1
    while d < HG_CHUNK:
        if reverse:
            shifted, ok = pltpu.roll(x, n - d, 0), pos < HG_CHUNK - d
        else:
            shifted, ok = pltpu.roll(x, d, 0), pos >= d
        x = x + jnp.where(ok, shifted, 0.0)
        d *= 2
    return x


def _hgrn_chunk(reverse, c, kk_ref, ones_ref, bmask_ref,
                o_ref, st_ref, qf_ref, vf_ref, b_ref, lhs_ref, res_ref):
    ones_bd = ones_ref[...]
    bmask = bmask_ref[...]
    sub_iota = lax.broadcasted_iota(jnp.int32, (SUB, HG_KEYS), 0)
    rows = pl.ds(c * HG_CHUNK, HG_CHUNK)
    b = b_ref[rows, :]
    q = qf_ref[rows, :]
    kk = kk_ref[rows, :]
    v = vf_ref[rows, :]
    b_end = b[0:1] if reverse else b[HG_CHUNK - 1:HG_CHUNK]

    off = 0
    for sb in range(NSUB):
        if reverse:
            t0, t1 = 0, SUB * (sb + 1)
            diag = slice(t1 - SUB, t1)
        else:
            t0, t1 = SUB * sb, HG_CHUNK
            diag = slice(0, SUB)
        n = t1 - t0
        q_r = q[t0:t1]
        b_r = b[t0:t1]
        for si in range(SUB):
            s = SUB * sb + si
            d = b_r - b[s:s + 1]
            keep = (sub_iota <= si) if reverse else (sub_iota >= si)
            d_diag = jnp.where(keep, d[diag], NEG_BIG)
            if reverse:
                d = jnp.concatenate([d[:n - SUB], d_diag], axis=0) if n > SUB else d_diag
            else:
                d = jnp.concatenate([d_diag, d[SUB:]], axis=0) if n > SUB else d_diag
            lhs_ref[off:off + n, :] = q_r * jnp.exp2(d) * kk[s:s + 1]
            off += n
            yield
    res_ref[...] = jnp.dot(lhs_ref[...].astype(BF16), ones_bd, preferred_element_type=F32)
    yield

    st = st_ref[...]
    qe = (q * jnp.exp2(b)).astype(BF16)
    o_inter = lax.dot_general(qe, st.astype(BF16), (((1,), (1,)), ((), ())),
                              preferred_element_type=F32)
    o_blk = [o_inter[SUB * tb:SUB * (tb + 1)] for tb in range(NSUB)]
    off = 0
    for sb in range(NSUB):
        t0 = 0 if reverse else SUB * sb
        n = SUB * (sb + 1) if reverse else HG_CHUNK - SUB * sb
        for si in range(SUB):
            s = SUB * sb + si
            contrib = res_ref[off:off + n, :] * v[s:s + 1]
            for j in range(n // SUB):
                tb = t0 // SUB + j
                o_blk[tb] = o_blk[tb] + contrib[SUB * j:SUB * (j + 1)]
            off += n
            yield
    o_ref[rows, :] = jnp.concatenate(o_blk, axis=0)

    kdec = (kk * jnp.exp2(b_end - b)).astype(BF16)
    upd = lax.dot_general(v.astype(BF16), kdec, (((0,), (0,)), ((), ())),
                          preferred_element_type=F32)
    st_ref[...] = st * jnp.exp2(b_end) + upd * bmask


SCAN_PIECES = 2 * HG_CHUNK + 1


def _interleave(main, n_main, others, n_other):
    done = 0
    for i, _ in enumerate(main):
        target = -(-(i + 1) * n_other // n_main)
        while done < target:
            for g in others:
                next(g, None)
            done += 1
    for g in others:
        for _ in g:
            pass

def _mix_kernel(q_ref, k_ref, v_ref,
                qf_in, vf_in, lff_in, kkf_in, qb_in, vb_in, lfb_in, kkb_in, ones_ref, bmask_ref,
                oa_out, of_out, ob_out,
                s_ref, st_ref, qf_ref, vf_ref, b_ref, lhs_ref, res_ref, *, ck, nh, nq, nchunk):
    @pl.when(pl.program_id(1) * nq + pl.program_id(2) == 0)
    def _():
        st_ref[...] = jnp.zeros_like(st_ref)

    for d, (q_in, v_in, lf_in) in enumerate(((qf_in, vf_in, lff_in), (qb_in, vb_in, lfb_in))):
        qf_ref[d] = q_in[...].astype(F32)
        vf_ref[d] = v_in[...].astype(F32)
        b_ref[d] = _chunk_cumsum(lf_in[...], reverse=bool(d))

    def scan_stream(reverse):
        d = int(reverse)
        kk_in, o_out = (kkb_in, ob_out) if reverse else (kkf_in, of_out)
        for ci in range(nchunk):
            yield from _hgrn_chunk(reverse, nchunk - 1 - ci if reverse else ci, kk_in, ones_ref,
                                   bmask_ref, o_out, st_ref.at[d], qf_ref.at[d], vf_ref.at[d],
                                   b_ref.at[d], lhs_ref.at[d], res_ref.at[d])

    def attn_stream(outs):
        for hd in range(nh):
            yield from _attn_head(hd, q_ref, k_ref, v_ref, s_ref, ck, outs)

    outs = []
    n_attn = nh * 2 * (k_ref.shape[0] // ck)
    n_scan = nchunk * SCAN_PIECES
    _interleave(attn_stream(outs), n_attn, [scan_stream(False), scan_stream(True)], n_scan)
    lane = lax.broadcasted_iota(jnp.int32, outs[0].shape, 1)
    for pr in range(nh // 2):
        oa_out[:, pr * 2 * MLA_V:(pr + 1) * 2 * MLA_V] = jnp.where(
            lane < MLA_V, outs[2 * pr], outs[2 * pr + 1]).astype(BF16)


def _mixers(q, k, v, hqig, logf, kk, consts, *, batch, seq, tq, ck, nh):
    T = q.shape[0]
    nq = seq // tq
    nsteps = (MLA_HEADS // nh) * nq
    tb = seq // nsteps
    fwd = lambda col: (lambda b, h, i: (b * nsteps + h * nq + i, col))
    bwd = lambda col: (lambda b, h, i: (b * nsteps + nsteps - 1 - (h * nq + i), col))
    blk = lambda m: pl.BlockSpec((tb, HG_KEYS), m)
    const = lambda a: pl.BlockSpec(a.shape, lambda b, h, i: (0, 0))
    ones_bd, bmask = consts
    scratch = [pltpu.VMEM((2, tq, seq), F32),
               pltpu.VMEM((2, HG_WIDTH, HG_KEYS), F32),
               pltpu.VMEM((2, tb, HG_KEYS), F32), pltpu.VMEM((2, tb, HG_KEYS), F32),
               pltpu.VMEM((2, tb, HG_KEYS), F32),
               pltpu.VMEM((2, LHS_ROWS, HG_KEYS), F32), pltpu.VMEM((2, LHS_ROWS, HG_KEYS), F32)]
    return pl.pallas_call(
        functools.partial(_mix_kernel, ck=ck, nh=nh, nq=nq, nchunk=tb // HG_CHUNK),
        grid=(batch, MLA_HEADS // nh, nq),
        in_specs=[
            pl.BlockSpec((tq, nh * HEAD_PAD), lambda b, h, i: (b * nq + i, h)),
            pl.BlockSpec((seq, nh * HEAD_PAD), lambda b, h, i: (b, h)),
            pl.BlockSpec((seq, nh * MLA_V), lambda b, h, i: (b, h)),
            blk(fwd(0)), blk(fwd(1)), blk(fwd(0)), blk(fwd(0)),
            blk(bwd(0)), blk(bwd(1)), blk(bwd(1)), blk(bwd(1)),
            const(ones_bd), const(bmask)],
        out_specs=[pl.BlockSpec((tq, nh * MLA_V), lambda b, h, i: (b * nq + i, h)),
                   blk(fwd(0)), blk(bwd(0))],
        out_shape=[jax.ShapeDtypeStruct((T, V_WIDTH), BF16),
                   jax.ShapeDtypeStruct((T, HG_WIDTH), F32), jax.ShapeDtypeStruct((T, HG_WIDTH), F32)],
        scratch_shapes=scratch,
        compiler_params=pltpu.CompilerParams(
            dimension_semantics=("arbitrary", "arbitrary", "arbitrary"),
            vmem_limit_bytes=VMEM_LIMIT),
        name="mixers",
    )(q, k, v, hqig, hqig, logf, kk, hqig, hqig, logf, kk, ones_bd, bmask)


FF_BLOCK = 1024


def _out_ffn_kernel(x_ref, oa_ref, ob_ref, of_ref, obw_ref, g_ref, go_ref, ones_ref,
                    wo_ref, gpm_ref, gpf_ref, w1_ref, w2_ref, gpo_ref, y_ref):
    o = of_ref[...] + obw_ref[...]
    ms = _seg_sum(o * o, ones_ref[...]) * (1.0 / 64)
    gate = g_ref[...].astype(F32)
    oc = o * lax.rsqrt(ms + EPS) * go_ref[...] * (gate * jax.nn.sigmoid(gate))
    a0, a1 = V_WIDTH, V_WIDTH + GM_WIDTH
    m = (jnp.dot(oa_ref[...], wo_ref[0:a0, :], preferred_element_type=F32)
         + jnp.dot(ob_ref[...], wo_ref[a0:a1, :], preferred_element_type=F32)
         + jnp.dot(oc.astype(BF16), wo_ref[a1:, :], preferred_element_type=F32))
    x1 = x_ref[...] + _rms(m, gpm_ref[...])
    h = _rms(x1, gpf_ref[...]).astype(BF16)
    acc = jnp.zeros(x1.shape, F32)
    for c in range(D_FF // FF_BLOCK):
        cs = slice(c * FF_BLOCK, (c + 1) * FF_BLOCK)
        a = jnp.maximum(jnp.dot(h, w1_ref[:, cs], preferred_element_type=F32), 0.0)
        acc = acc + jnp.dot((a * a).astype(BF16), w2_ref[cs, :], preferred_element_type=F32)
    y_ref[...] = x1 + _rms(acc, gpo_ref[...])


def _out_ffn(x2d, oa, ob, of, obw, hqig, wl, ones_bd, *, tm):
    T = x2d.shape[0]
    row = lambda i: (i, 0)
    const = lambda a: pl.BlockSpec(a.shape, lambda i: (0, 0), pipeline_mode=pl.Buffered(1))
    ws = [wl['hg_g_o'], ones_bd, wl['w_out'], wl['g_post_mix'], wl['g_pre_ffn'],
          wl['w_ff1'], wl['w_ff2'], wl['g_post_ffn']]
    return pl.pallas_call(
        _out_ffn_kernel,
        grid=(T // tm,),
        in_specs=[pl.BlockSpec((tm, D_MODEL), row), pl.BlockSpec((tm, V_WIDTH), row),
                  pl.BlockSpec((tm, GM_WIDTH), row), pl.BlockSpec((tm, HG_WIDTH), row),
                  pl.BlockSpec((tm, HG_WIDTH), row),
                  pl.BlockSpec((tm, HG_WIDTH), lambda i: (i, 2))] + [const(a) for a in ws],
        out_specs=pl.BlockSpec((tm, D_MODEL), row),
        out_shape=jax.ShapeDtypeStruct((T, D_MODEL), F32),
        compiler_params=pltpu.CompilerParams(dimension_semantics=("arbitrary",),
                                             vmem_limit_bytes=VMEM_LIMIT),
        name="out_ffn",
    )(x2d, oa, ob, of, obw, hqig, *ws)


def _rope_tables(seq):
    half = MLA_ROPE // 2
    inv = ROPE_BASE ** (-(jnp.arange(half, dtype=F32) / half))
    ang = jnp.arange(seq, dtype=F32)[:, None] * inv[None, :]
    cos, sin = jnp.cos(ang), jnp.sin(ang)
    pad = jnp.zeros((seq, HEAD_PAD - MLA_NOPE - MLA_ROPE), F32)
    ctab = jnp.concatenate([jnp.ones((seq, MLA_NOPE), F32), cos, cos, pad], axis=1)
    stab = jnp.concatenate([jnp.zeros((seq, MLA_NOPE), F32), -sin, sin, pad], axis=1)
    return ctab, stab


def _pack_weights(l, g_pre_mix, w_in, g_q_a, w_uq, g_kv_a, w_ukv, gm_ln_g, gm_ln_b, gm_w_s, gm_b_s,
                  hg_g_o, w_out, g_post_mix, g_pre_ffn, w_ff1, w_ff2, g_post_ffn):
    half = MLA_ROPE // 2
    swap = lambda a: jnp.concatenate([a[..., half:], a[..., :half]], axis=-1)
    row = lambda a: a[l][None, :]
    zpad = lambda n, k: jnp.zeros((n, k), F32)

    wi = w_in[l]
    o = np.cumsum((0, MLA_Q_RANK, MLA_KV_RANK, MLA_ROPE, GM_WIDTH, GM_WIDTH,
                   HG_KEYS, HG_KEYS, HG_KEYS, HG_WIDTH, HG_WIDTH))
    kr = wi[:, o[2]:o[3]]
    tail = HEAD_PAD - MLA_NOPE - MLA_ROPE
    kpe = jnp.concatenate([zpad(D_MODEL, MLA_NOPE), kr, zpad(D_MODEL, tail)], axis=1)
    kps = jnp.concatenate([zpad(D_MODEL, MLA_NOPE), swap(kr), zpad(D_MODEL, tail)], axis=1)
    w_in_p = jnp.concatenate([wi[:, o[0]:o[2]], kpe, kps, wi[:, o[3]:]], axis=1)

    wq = w_uq[l].reshape(MLA_Q_RANK, MLA_HEADS, MLA_NOPE + MLA_ROPE)
    qn, qr = wq[..., :MLA_NOPE], wq[..., MLA_NOPE:]
    zq = jnp.zeros((MLA_Q_RANK, MLA_HEADS, tail), F32)
    w_q = jnp.concatenate([qn, qr, zq], axis=-1).reshape(MLA_Q_RANK, QK_WIDTH)
    w_qs = jnp.concatenate([jnp.zeros_like(qn), swap(qr), zq], axis=-1).reshape(MLA_Q_RANK, QK_WIDTH)

    wkv = w_ukv[l].reshape(MLA_KV_RANK, MLA_HEADS, MLA_NOPE + MLA_V)
    kn, vv = wkv[..., :MLA_NOPE], wkv[..., MLA_NOPE:]
    zk = jnp.zeros((MLA_KV_RANK, MLA_HEADS, HEAD_PAD - MLA_NOPE), F32)
    w_k = jnp.concatenate([kn, zk], axis=-1).reshape(MLA_KV_RANK, QK_WIDTH)
    w_v = vv.reshape(MLA_KV_RANK, V_WIDTH)

    return dict(
        g_pre_mix=row(g_pre_mix), w_in=w_in_p.astype(BF16), g_q_a=row(g_q_a), w_q=w_q.astype(BF16),
        w_qs=w_qs.astype(BF16), g_kv_a=row(g_kv_a), w_k=w_k.astype(BF16), w_v=w_v.astype(BF16),
        gm_ln_g=row(gm_ln_g), gm_ln_b=row(gm_ln_b), gm_w_s=gm_w_s[l].astype(BF16),
        gm_b_s=jnp.broadcast_to(gm_b_s[l][:, None], (GM_CHUNK, GM_WIDTH)),
        hg_g_o=row(hg_g_o), w_out=w_out[l].astype(BF16), g_post_mix=row(g_post_mix),
        g_pre_ffn=row(g_pre_ffn), w_ff1=w_ff1[l].astype(BF16), w_ff2=w_ff2[l].astype(BF16),
        g_post_ffn=row(g_post_ffn))


def _scan_consts():
    k = np.arange(HG_KEYS)
    bd = (k[:, None] // HG_DK) == (k[None, :] // HG_DK)
    return jnp.asarray(bd, BF16), jnp.asarray(bd, F32)


def _trunk(x, layers, lbl, *, tm_in, tq, ck, nh, tm_out):
    batch, seq, _ = x.shape
    x2d = x.reshape(batch * seq, D_MODEL)
    tabs = _rope_tables(seq)
    consts = _scan_consts()
    ones_bd = consts[0]
    for l, wl in enumerate(layers):
        q, k, v, ob, hqig, logf, kk = _in_proj(x2d, wl, tabs, lbl, ones_bd, layer=l, seq=seq, tm=tm_in)
        oa, of, obw = _mixers(q, k, v, hqig, logf, kk, consts, batch=batch, seq=seq, tq=tq, ck=ck, nh=nh)
        x2d = _out_ffn(x2d, oa, ob, of, obw, hqig, wl, ones_bd, tm=tm_out)
    return x2d.reshape(batch, seq, D_MODEL)


def kernel(x_prompt, x_sample, hg_lb_logits, g_pre_mix, w_in, g_q_a, w_uq, g_kv_a, w_ukv, gm_ln_g, gm_ln_b,
           gm_w_s, gm_b_s, hg_g_o, w_out, g_post_mix, g_pre_ffn, w_ff1, w_ff2, g_post_ffn):
    depth = w_in.shape[0]
    layers = [_pack_weights(l, g_pre_mix, w_in, g_q_a, w_uq, g_kv_a, w_ukv, gm_ln_g, gm_ln_b, gm_w_s,
                            gm_b_s, hg_g_o, w_out, g_post_mix, g_pre_ffn, w_ff1, w_ff2, g_post_ffn)
              for l in range(depth)]
    lbl = hg_lb_logits.reshape(depth, 2 * HG_KEYS)
    y_prompt = _trunk(x_prompt, layers, lbl, tm_in=512, tq=512, ck=256, nh=8, tm_out=512)
    y_sample = _trunk(x_sample, layers, lbl, tm_in=512, tq=256, ck=512, nh=4, tm_out=512)
    return (y_prompt, y_sample)
```

```python
import functools

import numpy as np
import jax
import jax.numpy as jnp
from jax import lax
from jax.experimental import pallas as pl
from jax.experimental.pallas import tpu as pltpu

D_MODEL = 1024
MLA_HEADS = 8
MLA_NOPE = 64
MLA_ROPE = 32
MLA_V = 64
MLA_Q_RANK = D_MODEL // 4
MLA_KV_RANK = D_MODEL // 8
ROPE_BASE = 10000.0
GM_WIDTH = D_MODEL // 4
GM_GROUPS = 4
GM_CHUNK = 128
HG_HEADS = 4
HG_DK = 64
HG_KEYS = HG_HEADS * HG_DK
HG_WIDTH = HG_HEADS * 64
HG_CHUNK = 32
F_FLOOR = 1e-20
D_FF = 4 * D_MODEL
EPS = 1e-6

LANES = 128
HEAD_PAD = LANES
QK_WIDTH = MLA_HEADS * HEAD_PAD
V_WIDTH = MLA_HEADS * MLA_V
C_Q, C_KV, C_KPE, C_KPS, C_GU, C_GV, C_HQ, C_HF, C_HI, C_HG, C_END = (
    0, 256, 384, 512, 640, 896, 1152, 1408, 1920, 2176, 2432)
VMEM_LIMIT = 56 * 1024 * 1024
NEG_BIG = -1e30
IN_SUBTILE = 512
LOG2E = 1.4426950408889634

F32 = jnp.float32
BF16 = jnp.bfloat16


def _rms(x, g):
    return x * lax.rsqrt(jnp.mean(x * x, axis=-1, keepdims=True) + EPS) * g


def _gelu(x):
    return 0.5 * x * (1.0 + lax.erf(x * (0.5 ** 0.5)))


def _seg_sum(x, ones_bd):
    hi = x.astype(BF16)
    lo = (x - hi.astype(F32)).astype(BF16)
    return (jnp.dot(hi, ones_bd, preferred_element_type=F32)
            + jnp.dot(lo, ones_bd, preferred_element_type=F32))


def _in_proj_kernel(x_ref, gpre_ref, win_ref, gq_ref, wq_ref, wqs_ref, gkv_ref, wk_ref, wv_ref,
                    ctab_ref, stab_ref, lng_ref, lnb_ref, ws_ref, bs_ref, lbl_ref, ones_ref,
                    q_out, k_out, v_out, ob_out, hqig_out, logf_out, kk_out, *, layer, nsub):
    scale = (MLA_NOPE + MLA_ROPE) ** -0.5 * LOG2E
    ones_bd = ones_ref[...]
    lg = lbl_ref[...]
    e = jnp.exp(lg - jnp.max(lg, axis=0, keepdims=True))
    p = e / jnp.sum(e, axis=0, keepdims=True)
    cum = p[0:1]
    for i in range(1, layer + 1):
        cum = cum + p[i:i + 1]
    lb = jnp.clip(cum - p[0:1], 0.0, 0.999)

    n = x_ref.shape[0] // nsub
    for r in range(nsub):
        rs = slice(r * n, (r + 1) * n)
        h = _rms(x_ref[rs, :], gpre_ref[...]).astype(BF16)
        z = jnp.dot(h, win_ref[...], preferred_element_type=F32)
        ctab = ctab_ref[rs, :]
        stab = stab_ref[rs, :]

        cqn = _rms(z[:, C_Q:C_KV], gq_ref[...]).astype(BF16)
        q = jnp.dot(cqn, wq_ref[...], preferred_element_type=F32)
        qs = jnp.dot(cqn, wqs_ref[...], preferred_element_type=F32)
        for hd in range(MLA_HEADS):
            sl = slice(hd * HEAD_PAD, (hd + 1) * HEAD_PAD)
            q_out[rs, sl] = ((q[:, sl] * ctab + qs[:, sl] * stab) * scale).astype(BF16)

        ckvn = _rms(z[:, C_KV:C_KPE], gkv_ref[...]).astype(BF16)
        kn = jnp.dot(ckvn, wk_ref[...], preferred_element_type=F32)
        kpe = z[:, C_KPE:C_KPS] * ctab + z[:, C_KPS:C_GU] * stab
        for hd in range(MLA_HEADS):
            sl = slice(hd * HEAD_PAD, (hd + 1) * HEAD_PAD)
            k_out[rs, sl] = (kn[:, sl] + kpe).astype(BF16)
        v_out[rs, :] = jnp.dot(ckvn, wv_ref[...], preferred_element_type=F32).astype(BF16)

        u = _gelu(z[:, C_GU:C_GV])
        v = _gelu(z[:, C_GV:C_HQ])
        inv_n = 1.0 / (GM_WIDTH // GM_GROUPS)
        xc = v - _seg_sum(v, ones_bd) * inv_n
        var = _seg_sum(xc * xc, ones_bd) * inv_n
        vn = (xc * lax.rsqrt(var + EPS) * lng_ref[...] + lnb_ref[...]).astype(BF16)
        for c in range(n // GM_CHUNK):
            cr = slice(c * GM_CHUNK, (c + 1) * GM_CHUNK)
            vs = jnp.dot(ws_ref[...], vn[cr], preferred_element_type=F32) + bs_ref[...]
            ob_out[r * n + c * GM_CHUNK:r * n + (c + 1) * GM_CHUNK, :] = (u[cr] * vs).astype(BF16)

        xf = z[:, C_HF:C_HI]
        f = lb + (1.0 - lb) * jax.nn.sigmoid(xf)
        logf_out[rs, :] = jnp.log(jnp.maximum(f, F_FLOOR)) * LOG2E
        kk_out[rs, :] = jnp.log((1.0 - lb) * jax.nn.sigmoid(-xf)) * LOG2E
        hqig_out[rs, 0:HG_KEYS] = z[:, C_HQ:C_HF].astype(BF16)
        hqig_out[rs, HG_KEYS:] = z[:, C_HI:C_END].astype(BF16)


def _in_proj(x2d, wl, tabs, lbl, ones_bd, *, layer, seq, tm):
    T = x2d.shape[0]
    nt = T // tm
    npos = seq // tm
    const = lambda i: (0, 0)
    row = lambda i: (i, 0)
    pos = lambda i: (i % npos, 0)
    full = lambda a: pl.BlockSpec(a.shape, const)
    ins = [
        (x2d, pl.BlockSpec((tm, D_MODEL), row)),
        (wl['g_pre_mix'], None), (wl['w_in'], None), (wl['g_q_a'], None), (wl['w_q'], None),
        (wl['w_qs'], None), (wl['g_kv_a'], None), (wl['w_k'], None), (wl['w_v'], None),
        (tabs[0], pl.BlockSpec((tm, HEAD_PAD), pos)), (tabs[1], pl.BlockSpec((tm, HEAD_PAD), pos)),
        (wl['gm_ln_g'], None), (wl['gm_ln_b'], None), (wl['gm_w_s'], None), (wl['gm_b_s'], None),
        (lbl, None), (ones_bd, None),
    ]
    in_specs = [s if s is not None else full(a) for a, s in ins]
    outs = [(QK_WIDTH, BF16), (QK_WIDTH, BF16), (V_WIDTH, BF16), (GM_WIDTH, BF16),
            (3 * HG_KEYS, BF16), (2 * HG_KEYS, F32), (2 * HG_KEYS, F32)]
    return pl.pallas_call(
        functools.partial(_in_proj_kernel, layer=layer, nsub=max(1, tm // IN_SUBTILE)),
        grid=(nt,),
        in_specs=in_specs,
        out_specs=[pl.BlockSpec((tm, w), row) for w, _ in outs],
        out_shape=[jax.ShapeDtypeStruct((T, w), dt) for w, dt in outs],
        compiler_params=pltpu.CompilerParams(dimension_semantics=("arbitrary",),
                                             vmem_limit_bytes=VMEM_LIMIT),
        name="in_proj",
    )(*[a for a, _ in ins])


def _attn_head(hd, q_ref, k_ref, v_ref, s_ref, ck, outs):
    tq, seq = q_ref.shape[0], k_ref.shape[0]
    nck = seq // ck
    sl = slice(hd * HEAD_PAD, (hd + 1) * HEAD_PAD)
    vsl = slice((hd // 2) * 2 * MLA_V, (hd // 2 + 1) * 2 * MLA_V)
    buf = hd % 2
    q = q_ref[:, sl]
    mp = None
    for c in range(nck):
        cs = slice(c * ck, (c + 1) * ck)
        s = lax.dot_general(q, k_ref[cs, sl], (((1,), (1,)), ((), ())),
                            preferred_element_type=F32)
        s_ref[buf, :, cs] = s
        for j in range(ck // LANES):
            t = s[:, j * LANES:(j + 1) * LANES]
            mp = t if mp is None else jnp.maximum(mp, t)
        yield
    m = jnp.max(mp, axis=-1, keepdims=True)
    ones = jnp.ones((ck, LANES), BF16)
    acc = jnp.zeros((tq, 2 * MLA_V + LANES), F32)
    for c in range(nck):
        cs = slice(c * ck, (c + 1) * ck)
        p = jnp.exp2(s_ref[buf, :, cs] - m).astype(BF16)
        acc = acc + jnp.dot(p, jnp.concatenate([v_ref[cs, vsl], ones], axis=1),
                            preferred_element_type=F32)
        yield
    outs.append(acc[:, :2 * MLA_V] / acc[:, 2 * MLA_V:])


SUB = 8
NSUB = HG_CHUNK // SUB
ROWS_PER_SUB = tuple(SUB * (NSUB - sb) for sb in range(NSUB))
LHS_ROWS = SUB * sum(ROWS_PER_SUB)


def _chunk_cumsum(x, reverse):
    n = x.shape[0]
    pos = lax.broadcasted_iota(jnp.int32, x.shape, 0) & (HG_CHUNK - 1)
    d = 1
    while d < HG_CHUNK:
        if reverse:
            shifted, ok = pltpu.roll(x, n - d, 0), pos < HG_CHUNK - d
        else:
            shifted, ok = pltpu.roll(x, d, 0), pos >= d
        x = x + jnp.where(ok, shifted, 0.0)
        d *= 2
    return x


def _hgrn_chunk(reverse, c, ones_ref, bmask_ref,
                o_ref, st_ref, qf_ref, vf_ref, b_ref, g_ref, lhs_ref, res_ref):
    ones_bd = ones_ref[...]
    bmask = bmask_ref[...]
    sub_iota = lax.broadcasted_iota(jnp.int32, (SUB, HG_KEYS), 0)
    rows = pl.ds(c * HG_CHUNK, HG_CHUNK)
    b = b_ref[rows, :]
    q = qf_ref[rows, :]
    g = g_ref[rows, :]
    v = vf_ref[rows, :]
    b_end = b[0:1] if reverse else b[HG_CHUNK - 1:HG_CHUNK]

    off = 0
    for sb in range(NSUB):
        if reverse:
            t0, t1 = 0, SUB * (sb + 1)
            diag = slice(t1 - SUB, t1)
        else:
            t0, t1 = SUB * sb, HG_CHUNK
            diag = slice(0, SUB)
        n = t1 - t0
        q_r = q[t0:t1]
        b_r = b[t0:t1]
        for si in range(SUB):
            s = SUB * sb + si
            d = b_r - g[s:s + 1]
            keep = (sub_iota <= si) if reverse else (sub_iota >= si)
            d_diag = jnp.where(keep, d[diag], NEG_BIG)
            if reverse:
                d = jnp.concatenate([d[:n - SUB], d_diag], axis=0) if n > SUB else d_diag
            else:
                d = jnp.concatenate([d_diag, d[SUB:]], axis=0) if n > SUB else d_diag
            lhs_ref[off:off + n, :] = q_r * jnp.exp2(d)
            off += n
            yield
    res_ref[...] = jnp.dot(lhs_ref[...].astype(BF16), ones_bd, preferred_element_type=F32)
    yield

    st = st_ref[...]
    qe = (q * jnp.exp2(b)).astype(BF16)
    o_inter = lax.dot_general(qe, st.astype(BF16), (((1,), (1,)), ((), ())),
                              preferred_element_type=F32)
    o_blk = [o_inter[SUB * tb:SUB * (tb + 1)] for tb in range(NSUB)]
    off = 0
    for sb in range(NSUB):
        t0 = 0 if reverse else SUB * sb
        n = SUB * (sb + 1) if reverse else HG_CHUNK - SUB * sb
        for si in range(SUB):
            s = SUB * sb + si
            contrib = res_ref[off:off + n, :] * v[s:s + 1]
            for j in range(n // SUB):
                tb = t0 // SUB + j
                o_blk[tb] = o_blk[tb] + contrib[SUB * j:SUB * (j + 1)]
            off += n
            yield
    o_ref[rows, :] = jnp.concatenate(o_blk, axis=0)

    kdec = jnp.exp2(b_end - g).astype(BF16)
    upd = lax.dot_general(v.astype(BF16), kdec, (((0,), (0,)), ((), ())),
                          preferred_element_type=F32)
    st_ref[...] = st * jnp.exp2(b_end) + upd * bmask


SCAN_PIECES = 2 * HG_CHUNK + 1


def _interleave(main, n_main, others, n_other):
    done = 0
    for i, _ in enumerate(main):
        target = -(-(i + 1) * n_other // n_main)
        while done < target:
            for g in others:
                next(g, None)
            done += 1
    for g in others:
        for _ in g:
            pass

def _mix_kernel(q_ref, k_ref, v_ref,
                qf_in, vf_in, lff_in, kkf_in, qb_in, vb_in, lfb_in, kkb_in, ones_ref, bmask_ref,
                oa_out, of_out, ob_out,
                s_ref, st_ref, qf_ref, vf_ref, b_ref, g_ref, lhs_ref, res_ref, *, ck, nh, nq, nchunk):
    @pl.when(pl.program_id(1) * nq + pl.program_id(2) == 0)
    def _():
        st_ref[...] = jnp.zeros_like(st_ref)

    for d, (q_in, v_in, lf_in, lk_in) in enumerate(((qf_in, vf_in, lff_in, kkf_in),
                                                     (qb_in, vb_in, lfb_in, kkb_in))):
        qf_ref[d] = q_in[...].astype(F32)
        vf_ref[d] = v_in[...].astype(F32)
        b = _chunk_cumsum(lf_in[...], reverse=bool(d))
        b_ref[d] = b
        g_ref[d] = b - lk_in[...]

    def scan_stream(reverse):
        d = int(reverse)
        o_out = ob_out if reverse else of_out
        for ci in range(nchunk):
            yield from _hgrn_chunk(reverse, nchunk - 1 - ci if reverse else ci, ones_ref,
                                   bmask_ref, o_out, st_ref.at[d], qf_ref.at[d], vf_ref.at[d],
                                   b_ref.at[d], g_ref.at[d], lhs_ref.at[d], res_ref.at[d])

    def attn_stream(outs):
        for hd in range(nh):
            yield from _attn_head(hd, q_ref, k_ref, v_ref, s_ref, ck, outs)

    outs = []
    n_attn = nh * 2 * (k_ref.shape[0] // ck)
    n_scan = nchunk * SCAN_PIECES
    _interleave(attn_stream(outs), n_attn, [scan_stream(False), scan_stream(True)], n_scan)
    lane = lax.broadcasted_iota(jnp.int32, outs[0].shape, 1)
    for pr in range(nh // 2):
        oa_out[:, pr * 2 * MLA_V:(pr + 1) * 2 * MLA_V] = jnp.where(
            lane < MLA_V, outs[2 * pr], outs[2 * pr + 1]).astype(BF16)


def _mixers(q, k, v, hqig, logf, kk, consts, *, batch, seq, tq, ck, nh):
    T = q.shape[0]
    nq = seq // tq
    nsteps = (MLA_HEADS // nh) * nq
    tb = seq // nsteps
    fwd = lambda col: (lambda b, h, i: (b * nsteps + h * nq + i, col))
    bwd = lambda col: (lambda b, h, i: (b * nsteps + nsteps - 1 - (h * nq + i), col))
    blk = lambda m: pl.BlockSpec((tb, HG_KEYS), m)
    const = lambda a: pl.BlockSpec(a.shape, lambda b, h, i: (0, 0))
    ones_bd, bmask = consts
    scratch = [pltpu.VMEM((2, tq, seq), F32),
               pltpu.VMEM((2, HG_WIDTH, HG_KEYS), F32),
               pltpu.VMEM((2, tb, HG_KEYS), F32), pltpu.VMEM((2, tb, HG_KEYS), F32),
               pltpu.VMEM((2, tb, HG_KEYS), F32), pltpu.VMEM((2, tb, HG_KEYS), F32),
               pltpu.VMEM((2, LHS_ROWS, HG_KEYS), F32), pltpu.VMEM((2, LHS_ROWS, HG_KEYS), F32)]
    return pl.pallas_call(
        functools.partial(_mix_kernel, ck=ck, nh=nh, nq=nq, nchunk=tb // HG_CHUNK),
        grid=(batch, MLA_HEADS // nh, nq),
        in_specs=[
            pl.BlockSpec((tq, nh * HEAD_PAD), lambda b, h, i: (b * nq + i, h)),
            pl.BlockSpec((seq, nh * HEAD_PAD), lambda b, h, i: (b, h)),
            pl.BlockSpec((seq, nh * MLA_V), lambda b, h, i: (b, h)),
            blk(fwd(0)), blk(fwd(1)), blk(fwd(0)), blk(fwd(0)),
            blk(bwd(0)), blk(bwd(1)), blk(bwd(1)), blk(bwd(1)),
            const(ones_bd), const(bmask)],
        out_specs=[pl.BlockSpec((tq, nh * MLA_V), lambda b, h, i: (b * nq + i, h)),
                   blk(fwd(0)), blk(bwd(0))],
        out_shape=[jax.ShapeDtypeStruct((T, V_WIDTH), BF16),
                   jax.ShapeDtypeStruct((T, HG_WIDTH), F32), jax.ShapeDtypeStruct((T, HG_WIDTH), F32)],
        scratch_shapes=scratch,
        compiler_params=pltpu.CompilerParams(
            dimension_semantics=("arbitrary", "arbitrary", "arbitrary"),
            vmem_limit_bytes=VMEM_LIMIT),
        name="mixers",
    )(q, k, v, hqig, hqig, logf, kk, hqig, hqig, logf, kk, ones_bd, bmask)


FF_BLOCK = 1024
FFN_SUBTILE = 512


def _out_ffn_kernel(x_ref, oa_ref, ob_ref, of_ref, obw_ref, g_ref, go_ref, ones_ref,
                    wo_ref, gpm_ref, gpf_ref, w1_ref, w2_ref, gpo_ref, y_ref, *, nsub):
    n = x_ref.shape[0] // nsub
    for r in range(nsub):
        rs = slice(r * n, (r + 1) * n)
        o = of_ref[rs, :] + obw_ref[rs, :]
        ms = _seg_sum(o * o, ones_ref[...]) * (1.0 / 64)
        gate = g_ref[rs, :].astype(F32)
        oc = o * lax.rsqrt(ms + EPS) * go_ref[...] * (gate * jax.nn.sigmoid(gate))
        a0, a1 = V_WIDTH, V_WIDTH + GM_WIDTH
        m = (jnp.dot(oa_ref[rs, :], wo_ref[0:a0, :], preferred_element_type=F32)
             + jnp.dot(ob_ref[rs, :], wo_ref[a0:a1, :], preferred_element_type=F32)
             + jnp.dot(oc.astype(BF16), wo_ref[a1:, :], preferred_element_type=F32))
        x1 = x_ref[rs, :] + _rms(m, gpm_ref[...])
        h = _rms(x1, gpf_ref[...]).astype(BF16)
        acc = jnp.zeros(x1.shape, F32)
        for c in range(D_FF // FF_BLOCK):
            cs = slice(c * FF_BLOCK, (c + 1) * FF_BLOCK)
            a = jnp.maximum(jnp.dot(h, w1_ref[:, cs], preferred_element_type=F32), 0.0)
            acc = acc + jnp.dot((a * a).astype(BF16), w2_ref[cs, :], preferred_element_type=F32)
        y_ref[rs, :] = x1 + _rms(acc, gpo_ref[...])


def _out_ffn(x2d, oa, ob, of, obw, hqig, wl, ones_bd, *, tm):
    T = x2d.shape[0]
    row = lambda i: (i, 0)
    const = lambda a: pl.BlockSpec(a.shape, lambda i: (0, 0), pipeline_mode=pl.Buffered(1))
    ws = [wl['hg_g_o'], ones_bd, wl['w_out'], wl['g_post_mix'], wl['g_pre_ffn'],
          wl['w_ff1'], wl['w_ff2'], wl['g_post_ffn']]
    return pl.pallas_call(
        functools.partial(_out_ffn_kernel, nsub=max(1, tm // FFN_SUBTILE)),
        grid=(T // tm,),
        in_specs=[pl.BlockSpec((tm, D_MODEL), row), pl.BlockSpec((tm, V_WIDTH), row),
                  pl.BlockSpec((tm, GM_WIDTH), row), pl.BlockSpec((tm, HG_WIDTH), row),
                  pl.BlockSpec((tm, HG_WIDTH), row),
                  pl.BlockSpec((tm, HG_WIDTH), lambda i: (i, 2))] + [const(a) for a in ws],
        out_specs=pl.BlockSpec((tm, D_MODEL), row),
        out_shape=jax.ShapeDtypeStruct((T, D_MODEL), F32),
        compiler_params=pltpu.CompilerParams(dimension_semantics=("arbitrary",),
                                             vmem_limit_bytes=VMEM_LIMIT),
        name="out_ffn",
    )(x2d, oa, ob, of, obw, hqig, *ws)


def _rope_tables(seq):
    half = MLA_ROPE // 2
    inv = ROPE_BASE ** (-(jnp.arange(half, dtype=F32) / half))
    ang = jnp.arange(seq, dtype=F32)[:, None] * inv[None, :]
    cos, sin = jnp.cos(ang), jnp.sin(ang)
    pad = jnp.zeros((seq, HEAD_PAD - MLA_NOPE - MLA_ROPE), F32)
    ctab = jnp.concatenate([jnp.ones((seq, MLA_NOPE), F32), cos, cos, pad], axis=1)
    stab = jnp.concatenate([jnp.zeros((seq, MLA_NOPE), F32), -sin, sin, pad], axis=1)
    return ctab, stab


def _pack_weights(l, g_pre_mix, w_in, g_q_a, w_uq, g_kv_a, w_ukv, gm_ln_g, gm_ln_b, gm_w_s, gm_b_s,
                  hg_g_o, w_out, g_post_mix, g_pre_ffn, w_ff1, w_ff2, g_post_ffn):
    half = MLA_ROPE // 2
    swap = lambda a: jnp.concatenate([a[..., half:], a[..., :half]], axis=-1)
    row = lambda a: a[l][None, :]
    zpad = lambda n, k: jnp.zeros((n, k), F32)

    wi = w_in[l]
    o = np.cumsum((0, MLA_Q_RANK, MLA_KV_RANK, MLA_ROPE, GM_WIDTH, GM_WIDTH,
                   HG_KEYS, HG_KEYS, HG_KEYS, HG_WIDTH, HG_WIDTH))
    kr = wi[:, o[2]:o[3]]
    tail = HEAD_PAD - MLA_NOPE - MLA_ROPE
    kpe = jnp.concatenate([zpad(D_MODEL, MLA_NOPE), kr, zpad(D_MODEL, tail)], axis=1)
    kps = jnp.concatenate([zpad(D_MODEL, MLA_NOPE), swap(kr), zpad(D_MODEL, tail)], axis=1)
    w_in_p = jnp.concatenate([wi[:, o[0]:o[2]], kpe, kps, wi[:, o[3]:]], axis=1)

    wq = w_uq[l].reshape(MLA_Q_RANK, MLA_HEADS, MLA_NOPE + MLA_ROPE)
    qn, qr = wq[..., :MLA_NOPE], wq[..., MLA_NOPE:]
    zq = jnp.zeros((MLA_Q_RANK, MLA_HEADS, tail), F32)
    w_q = jnp.concatenate([qn, qr, zq], axis=-1).reshape(MLA_Q_RANK, QK_WIDTH)
    w_qs = jnp.concatenate([jnp.zeros_like(qn), swap(qr), zq], axis=-1).reshape(MLA_Q_RANK, QK_WIDTH)

    wkv = w_ukv[l].reshape(MLA_KV_RANK, MLA_HEADS, MLA_NOPE + MLA_V)
    kn, vv = wkv[..., :MLA_NOPE], wkv[..., MLA_NOPE:]
    zk = jnp.zeros((MLA_KV_RANK, MLA_HEADS, HEAD_PAD - MLA_NOPE), F32)
    w_k = jnp.concatenate([kn, zk], axis=-1).reshape(MLA_KV_RANK, QK_WIDTH)
    w_v = vv.reshape(MLA_KV_RANK, V_WIDTH)

    return dict(
        g_pre_mix=row(g_pre_mix), w_in=w_in_p.astype(BF16), g_q_a=row(g_q_a), w_q=w_q.astype(BF16),
        w_qs=w_qs.astype(BF16), g_kv_a=row(g_kv_a), w_k=w_k.astype(BF16), w_v=w_v.astype(BF16),
        gm_ln_g=row(gm_ln_g), gm_ln_b=row(gm_ln_b), gm_w_s=gm_w_s[l].astype(BF16),
        gm_b_s=jnp.broadcast_to(gm_b_s[l][:, None], (GM_CHUNK, GM_WIDTH)),
        hg_g_o=row(hg_g_o), w_out=w_out[l].astype(BF16), g_post_mix=row(g_post_mix),
        g_pre_ffn=row(g_pre_ffn), w_ff1=w_ff1[l].astype(BF16), w_ff2=w_ff2[l].astype(BF16),
        g_post_ffn=row(g_post_ffn))


def _scan_consts():
    k = np.arange(HG_KEYS)
    bd = (k[:, None] // HG_DK) == (k[None, :] // HG_DK)
    return jnp.asarray(bd, BF16), jnp.asarray(bd, F32)


def _trunk(x, layers, lbl, *, tm_in, tq, ck, nh, tm_out):
    batch, seq, _ = x.shape
    x2d = x.reshape(batch * seq, D_MODEL)
    tabs = _rope_tables(seq)
    consts = _scan_consts()
    ones_bd = consts[0]
    for l, wl in enumerate(layers):
        q, k, v, ob, hqig, logf, kk = _in_proj(x2d, wl, tabs, lbl, ones_bd, layer=l, seq=seq, tm=tm_in)
        oa, of, obw = _mixers(q, k, v, hqig, logf, kk, consts, batch=batch, seq=seq, tq=tq, ck=ck, nh=nh)
        x2d = _out_ffn(x2d, oa, ob, of, obw, hqig, wl, ones_bd, tm=tm_out)
    return x2d.reshape(batch, seq, D_MODEL)


def kernel(x_prompt, x_sample, hg_lb_logits, g_pre_mix, w_in, g_q_a, w_uq, g_kv_a, w_ukv, gm_ln_g, gm_ln_b,
           gm_w_s, gm_b_s, hg_g_o, w_out, g_post_mix, g_pre_ffn, w_ff1, w_ff2, g_post_ffn):
    depth = w_in.shape[0]
    layers = [_pack_weights(l, g_pre_mix, w_in, g_q_a, w_uq, g_kv_a, w_ukv, gm_ln_g, gm_ln_b, gm_w_s,
                            gm_b_s, hg_g_o, w_out, g_post_mix, g_pre_ffn, w_ff1, w_ff2, g_post_ffn)
              for l in range(depth)]
    lbl = hg_lb_logits.reshape(depth, 2 * HG_KEYS)
    y_prompt = _trunk(x_prompt, layers, lbl, tm_in=1024, tq=512, ck=256, nh=8, tm_out=1024)
    y_sample = _trunk(x_sample, layers, lbl, tm_in=1024, tq=256, ck=512, nh=4, tm_out=1024)
    return (y_prompt, y_sample)
```

```python
import functools

import numpy as np
import jax
import jax.numpy as jnp
from jax import lax
from jax.experimental import pallas as pl
from jax.experimental.pallas import tpu as pltpu

D_MODEL = 1024
MLA_HEADS = 8
MLA_NOPE = 64
MLA_ROPE = 32
MLA_V = 64
MLA_Q_RANK = D_MODEL // 4
MLA_KV_RANK = D_MODEL // 8
ROPE_BASE = 10000.0
GM_WIDTH = D_MODEL // 4
GM_GROUPS = 4
GM_CHUNK = 128
HG_HEADS = 4
HG_DK = 64
HG_KEYS = HG_HEADS * HG_DK
HG_WIDTH = HG_HEADS * 64
HG_CHUNK = 32
F_FLOOR = 1e-20
D_FF = 4 * D_MODEL
EPS = 1e-6

LANES = 128
HEAD_PAD = LANES
QK_WIDTH = MLA_HEADS * HEAD_PAD
V_WIDTH = MLA_HEADS * MLA_V
C_Q, C_KV, C_KPE, C_KPS, C_GU, C_GV, C_HQ, C_HF, C_HI, C_HG, C_END = (
    0, 256, 384, 512, 640, 896, 1152, 1408, 1920, 2176, 2432)
VMEM_LIMIT = 56 * 1024 * 1024
KV_DOUBLE_BUFFER_BUDGET = 40 * 1024 * 1024
NEG_BIG = -1e30
IN_SUBTILE = 512
LOG2E = 1.4426950408889634

F32 = jnp.float32
BF16 = jnp.bfloat16


def _rms(x, g):
    return x * lax.rsqrt(jnp.mean(x * x, axis=-1, keepdims=True) + EPS) * g


def _gelu(x):
    return 0.5 * x * (1.0 + lax.erf(x * (0.5 ** 0.5)))


def _seg_sum(x, ones_bd):
    hi = x.astype(BF16)
    lo = (x - hi.astype(F32)).astype(BF16)
    return (jnp.dot(hi, ones_bd, preferred_element_type=F32)
            + jnp.dot(lo, ones_bd, preferred_element_type=F32))


def _in_proj_kernel(x_ref, gpre_ref, win_ref, gq_ref, wq_ref, wqs_ref, gkv_ref, wk_ref, wv_ref,
                    ctab_ref, stab_ref, lng_ref, lnb_ref, ws_ref, bs_ref, lbl_ref, ones_ref,
                    q_out, k_out, v_out, ob_out, hqig_out, logf_out, kk_out, *, layer, nsub):
    scale = (MLA_NOPE + MLA_ROPE) ** -0.5 * LOG2E
    ones_bd = ones_ref[...]
    lg = lbl_ref[...]
    e = jnp.exp(lg - jnp.max(lg, axis=0, keepdims=True))
    p = e / jnp.sum(e, axis=0, keepdims=True)
    cum = p[0:1]
    for i in range(1, layer + 1):
        cum = cum + p[i:i + 1]
    lb = jnp.clip(cum - p[0:1], 0.0, 0.999)

    n = x_ref.shape[0] // nsub
    for r in range(nsub):
        rs = slice(r * n, (r + 1) * n)
        h = _rms(x_ref[rs, :], gpre_ref[...]).astype(BF16)
        z = jnp.dot(h, win_ref[...], preferred_element_type=F32)
        ctab = ctab_ref[rs, :]
        stab = stab_ref[rs, :]

        cqn = _rms(z[:, C_Q:C_KV], gq_ref[...]).astype(BF16)
        q = jnp.dot(cqn, wq_ref[...], preferred_element_type=F32)
        qs = jnp.dot(cqn, wqs_ref[...], preferred_element_type=F32)
        for hd in range(MLA_HEADS):
            sl = slice(hd * HEAD_PAD, (hd + 1) * HEAD_PAD)
            q_out[rs, sl] = ((q[:, sl] * ctab + qs[:, sl] * stab) * scale).astype(BF16)

        ckvn = _rms(z[:, C_KV:C_KPE], gkv_ref[...]).astype(BF16)
        kn = jnp.dot(ckvn, wk_ref[...], preferred_element_type=F32)
        kpe = z[:, C_KPE:C_KPS] * ctab + z[:, C_KPS:C_GU] * stab
        for hd in range(MLA_HEADS):
            sl = slice(hd * HEAD_PAD, (hd + 1) * HEAD_PAD)
            k_out[rs, sl] = (kn[:, sl] + kpe).astype(BF16)
        v_out[rs, :] = jnp.dot(ckvn, wv_ref[...], preferred_element_type=F32).astype(BF16)

        u = _gelu(z[:, C_GU:C_GV])
        v = _gelu(z[:, C_GV:C_HQ])
        inv_n = 1.0 / (GM_WIDTH // GM_GROUPS)
        xc = v - _seg_sum(v, ones_bd) * inv_n
        var = _seg_sum(xc * xc, ones_bd) * inv_n
        vn = (xc * lax.rsqrt(var + EPS) * lng_ref[...] + lnb_ref[...]).astype(BF16)
        for c in range(n // GM_CHUNK):
            cr = slice(c * GM_CHUNK, (c + 1) * GM_CHUNK)
            vs = jnp.dot(ws_ref[...], vn[cr], preferred_element_type=F32) + bs_ref[...]
            ob_out[r * n + c * GM_CHUNK:r * n + (c + 1) * GM_CHUNK, :] = (u[cr] * vs).astype(BF16)

        xf = z[:, C_HF:C_HI]
        f = lb + (1.0 - lb) * jax.nn.sigmoid(xf)
        logf_out[rs, :] = jnp.log(jnp.maximum(f, F_FLOOR)) * LOG2E
        kk_out[rs, :] = jnp.log((1.0 - lb) * jax.nn.sigmoid(-xf)) * LOG2E
        hqig_out[rs, 0:HG_KEYS] = z[:, C_HQ:C_HF].astype(BF16)
        hqig_out[rs, HG_KEYS:] = z[:, C_HI:C_END].astype(BF16)


def _in_proj(x2d, wl, tabs, lbl, ones_bd, *, layer, seq, tm):
    T = x2d.shape[0]
    nt = T // tm
    npos = seq // tm
    const = lambda i: (0, 0)
    row = lambda i: (i, 0)
    pos = lambda i: (i % npos, 0)
    full = lambda a: pl.BlockSpec(a.shape, const)
    ins = [
        (x2d, pl.BlockSpec((tm, D_MODEL), row)),
        (wl['g_pre_mix'], None), (wl['w_in'], None), (wl['g_q_a'], None), (wl['w_q'], None),
        (wl['w_qs'], None), (wl['g_kv_a'], None), (wl['w_k'], None), (wl['w_v'], None),
        (tabs[0], pl.BlockSpec((tm, HEAD_PAD), pos)), (tabs[1], pl.BlockSpec((tm, HEAD_PAD), pos)),
        (wl['gm_ln_g'], None), (wl['gm_ln_b'], None), (wl['gm_w_s'], None), (wl['gm_b_s'], None),
        (lbl, None), (ones_bd, None),
    ]
    in_specs = [s if s is not None else full(a) for a, s in ins]
    outs = [(QK_WIDTH, BF16), (QK_WIDTH, BF16), (V_WIDTH, BF16), (GM_WIDTH, BF16),
            (3 * HG_KEYS, BF16), (2 * HG_KEYS, F32), (2 * HG_KEYS, F32)]
    return pl.pallas_call(
        functools.partial(_in_proj_kernel, layer=layer, nsub=max(1, tm // IN_SUBTILE)),
        grid=(nt,),
        in_specs=in_specs,
        out_specs=[pl.BlockSpec((tm, w), row) for w, _ in outs],
        out_shape=[jax.ShapeDtypeStruct((T, w), dt) for w, dt in outs],
        compiler_params=pltpu.CompilerParams(dimension_semantics=("arbitrary",),
                                             vmem_limit_bytes=VMEM_LIMIT),
        name="in_proj",
    )(*[a for a, _ in ins])


def _attn_head(hd, q_ref, k_ref, v_ref, s_ref, ck, outs):
    tq, seq = q_ref.shape[0], k_ref.shape[0]
    nck = seq // ck
    sl = slice(hd * HEAD_PAD, (hd + 1) * HEAD_PAD)
    vsl = slice((hd // 2) * 2 * MLA_V, (hd // 2 + 1) * 2 * MLA_V)
    buf = hd % 2
    q = q_ref[:, sl]
    mp = None
    for c in range(nck):
        cs = slice(c * ck, (c + 1) * ck)
        s = lax.dot_general(q, k_ref[cs, sl], (((1,), (1,)), ((), ())),
                            preferred_element_type=F32)
        s_ref[buf, :, cs] = s
        for j in range(ck // LANES):
            t = s[:, j * LANES:(j + 1) * LANES]
            mp = t if mp is None else jnp.maximum(mp, t)
        yield
    m = jnp.max(mp, axis=-1, keepdims=True)
    ones = jnp.ones((ck, LANES), BF16)
    acc = jnp.zeros((tq, 2 * MLA_V + LANES), F32)
    for c in range(nck):
        cs = slice(c * ck, (c + 1) * ck)
        p = jnp.exp2(s_ref[buf, :, cs] - m).astype(BF16)
        acc = acc + jnp.dot(p, jnp.concatenate([v_ref[cs, vsl], ones], axis=1),
                            preferred_element_type=F32)
        yield
    outs.append(acc[:, :2 * MLA_V] / acc[:, 2 * MLA_V:])


SUB = 8
NSUB = HG_CHUNK // SUB
ROWS_PER_SUB = tuple(SUB * (NSUB - sb) for sb in range(NSUB))
LHS_ROWS = SUB * sum(ROWS_PER_SUB)


def _chunk_cumsum(x, reverse):
    n = x.shape[0]
    pos = lax.broadcasted_iota(jnp.int32, x.shape, 0) & (HG_CHUNK - 1)
    d = 1
    while d < HG_CHUNK:
        if reverse:
            shifted, ok = pltpu.roll(x, n - d, 0), pos < HG_CHUNK - d
        else:
            shifted, ok = pltpu.roll(x, d, 0), pos >= d
        x = x + jnp.where(ok, shifted, 0.0)
        d *= 2
    return x


def _hgrn_chunk(reverse, c, ones_ref, bmask_ref,
                o_ref, st_ref, qf_ref, vf_ref, b_ref, g_ref, lhs_ref, res_ref):
    ones_bd = ones_ref[...]
    bmask = bmask_ref[...]
    sub_iota = lax.broadcasted_iota(jnp.int32, (SUB, HG_KEYS), 0)
    rows = pl.ds(c * HG_CHUNK, HG_CHUNK)
    b = b_ref[rows, :]
    q = qf_ref[rows, :]
    g = g_ref[rows, :]
    v = vf_ref[rows, :]
    b_end = b[0:1] if reverse else b[HG_CHUNK - 1:HG_CHUNK]

    off = 0
    for sb in range(NSUB):
        if reverse:
            t0, t1 = 0, SUB * (sb + 1)
            diag = slice(t1 - SUB, t1)
        else:
            t0, t1 = SUB * sb, HG_CHUNK
            diag = slice(0, SUB)
        n = t1 - t0
        q_r = q[t0:t1]
        b_r = b[t0:t1]
        for si in range(SUB):
            s = SUB * sb + si
            d = b_r - g[s:s + 1]
            keep = (sub_iota <= si) if reverse else (sub_iota >= si)
            d_diag = jnp.where(keep, d[diag], NEG_BIG)
            if reverse:
                d = jnp.concatenate([d[:n - SUB], d_diag], axis=0) if n > SUB else d_diag
            else:
                d = jnp.concatenate([d_diag, d[SUB:]], axis=0) if n > SUB else d_diag
            lhs_ref[off:off + n, :] = q_r * jnp.exp2(d)
            off += n
            yield
    res_ref[...] = jnp.dot(lhs_ref[...].astype(BF16), ones_bd, preferred_element_type=F32)
    yield

    halves = [slice(pr * LANES, (pr + 1) * LANES) for pr in range(2)]
    st = [st_ref[pr] for pr in range(2)]
    qe = (q * jnp.exp2(b)).astype(BF16)
    o_inter = jnp.concatenate(
        [lax.dot_general(qe[:, hv], st[pr].astype(BF16), (((1,), (1,)), ((), ())),
                         preferred_element_type=F32) for pr, hv in enumerate(halves)], axis=1)
    o_blk = [o_inter[SUB * tb:SUB * (tb + 1)] for tb in range(NSUB)]
    off = 0
    for sb in range(NSUB):
        t0 = 0 if reverse else SUB * sb
        n = SUB * (sb + 1) if reverse else HG_CHUNK - SUB * sb
        for si in range(SUB):
            s = SUB * sb + si
            contrib = res_ref[off:off + n, :] * v[s:s + 1]
            for j in range(n // SUB):
                tb = t0 // SUB + j
                o_blk[tb] = o_blk[tb] + contrib[SUB * j:SUB * (j + 1)]
            off += n
            yield
    o_ref[rows, :] = jnp.concatenate(o_blk, axis=0)

    kdec = jnp.exp2(b_end - g).astype(BF16)
    vb = v.astype(BF16)
    dec = jnp.exp2(b_end)
    for pr, hv in enumerate(halves):
        upd = lax.dot_general(vb[:, hv], kdec[:, hv], (((0,), (0,)), ((), ())),
                              preferred_element_type=F32)
        st_ref[pr] = st[pr] * dec[:, hv] + upd * bmask


SCAN_PIECES = 2 * HG_CHUNK + 1


def _interleave(main, n_main, others, n_other):
    done = 0
    for i, _ in enumerate(main):
        target = -(-(i + 1) * n_other // n_main)
        while done < target:
            for g in others:
                next(g, None)
            done += 1
    for g in others:
        for _ in g:
            pass

def _mix_kernel(q_ref, k_ref, v_ref,
                qf_in, vf_in, lff_in, kkf_in, qb_in, vb_in, lfb_in, kkb_in, ones_ref, bmask_ref,
                oa_out, of_out, ob_out,
                s_ref, st_ref, qf_ref, vf_ref, b_ref, g_ref, lhs_ref, res_ref, *, ck, nh, nq, nchunk):
    @pl.when(pl.program_id(1) * nq + pl.program_id(2) == 0)
    def _():
        st_ref[...] = jnp.zeros_like(st_ref)

    for d, (q_in, v_in, lf_in, lk_in) in enumerate(((qf_in, vf_in, lff_in, kkf_in),
                                                     (qb_in, vb_in, lfb_in, kkb_in))):
        qf_ref[d] = q_in[...].astype(F32)
        vf_ref[d] = v_in[...].astype(F32)
        b = _chunk_cumsum(lf_in[...], reverse=bool(d))
        b_ref[d] = b
        g_ref[d] = b - lk_in[...]

    def scan_stream(reverse):
        d = int(reverse)
        o_out = ob_out if reverse else of_out
        for ci in range(nchunk):
            yield from _hgrn_chunk(reverse, nchunk - 1 - ci if reverse else ci, ones_ref,
                                   bmask_ref, o_out, st_ref.at[d], qf_ref.at[d], vf_ref.at[d],
                                   b_ref.at[d], g_ref.at[d], lhs_ref.at[d], res_ref.at[d])

    def attn_stream(outs):
        for hd in range(nh):
            yield from _attn_head(hd, q_ref, k_ref, v_ref, s_ref, ck, outs)

    outs = []
    n_attn = nh * 2 * (k_ref.shape[0] // ck)
    n_scan = nchunk * SCAN_PIECES
    _interleave(attn_stream(outs), n_attn, [scan_stream(False), scan_stream(True)], n_scan)
    lane = lax.broadcasted_iota(jnp.int32, outs[0].shape, 1)
    for pr in range(nh // 2):
        oa_out[:, pr * 2 * MLA_V:(pr + 1) * 2 * MLA_V] = jnp.where(
            lane < MLA_V, outs[2 * pr], outs[2 * pr + 1]).astype(BF16)


def _mixers(q, k, v, hqig, logf, kk, consts, *, batch, seq, tq, ck, nh):
    T = q.shape[0]
    nq = seq // tq
    nsteps = (MLA_HEADS // nh) * nq
    tb = seq // nsteps
    fwd = lambda col: (lambda b, h, i: (b * nsteps + h * nq + i, col))
    bwd = lambda col: (lambda b, h, i: (b * nsteps + nsteps - 1 - (h * nq + i), col))
    blk = lambda m: pl.BlockSpec((tb, HG_KEYS), m)
    const = lambda a: pl.BlockSpec(a.shape, lambda b, h, i: (0, 0))
    ones_bd, bmask = consts
    kv_bytes = seq * nh * (HEAD_PAD + MLA_V) * 2
    s_bytes = 2 * tq * seq * 4
    kv_mode = {} if s_bytes + 2 * kv_bytes <= KV_DOUBLE_BUFFER_BUDGET else dict(pipeline_mode=pl.Buffered(1))
    scratch = [pltpu.VMEM((2, tq, seq), F32),
               pltpu.VMEM((2, HG_HEADS // 2, LANES, LANES), F32),
               pltpu.VMEM((2, tb, HG_KEYS), F32), pltpu.VMEM((2, tb, HG_KEYS), F32),
               pltpu.VMEM((2, tb, HG_KEYS), F32), pltpu.VMEM((2, tb, HG_KEYS), F32),
               pltpu.VMEM((2, LHS_ROWS, HG_KEYS), F32), pltpu.VMEM((2, LHS_ROWS, HG_KEYS), F32)]
    return pl.pallas_call(
        functools.partial(_mix_kernel, ck=ck, nh=nh, nq=nq, nchunk=tb // HG_CHUNK),
        grid=(batch, MLA_HEADS // nh, nq),
        in_specs=[
            pl.BlockSpec((tq, nh * HEAD_PAD), lambda b, h, i: (b * nq + i, h)),
            pl.BlockSpec((seq, nh * HEAD_PAD), lambda b, h, i: (b, h), **kv_mode),
            pl.BlockSpec((seq, nh * MLA_V), lambda b, h, i: (b, h), **kv_mode),
            blk(fwd(0)), blk(fwd(1)), blk(fwd(0)), blk(fwd(0)),
            blk(bwd(0)), blk(bwd(1)), blk(bwd(1)), blk(bwd(1)),
            const(ones_bd), const(bmask)],
        out_specs=[pl.BlockSpec((tq, nh * MLA_V), lambda b, h, i: (b * nq + i, h)),
                   blk(fwd(0)), blk(bwd(0))],
        out_shape=[jax.ShapeDtypeStruct((T, V_WIDTH), BF16),
                   jax.ShapeDtypeStruct((T, HG_WIDTH), F32), jax.ShapeDtypeStruct((T, HG_WIDTH), F32)],
        scratch_shapes=scratch,
        compiler_params=pltpu.CompilerParams(
            dimension_semantics=("arbitrary", "arbitrary", "arbitrary"),
            vmem_limit_bytes=VMEM_LIMIT),
        name="mixers",
    )(q, k, v, hqig, hqig, logf, kk, hqig, hqig, logf, kk, ones_bd, bmask)


FF_BLOCK = 1024
FFN_SUBTILE = 512


def _out_ffn_kernel(x_ref, oa_ref, ob_ref, of_ref, obw_ref, g_ref, go_ref, ones_ref,
                    wo_ref, gpm_ref, gpf_ref, w1_ref, w2_ref, gpo_ref, y_ref, *, nsub):
    n = x_ref.shape[0] // nsub
    for r in range(nsub):
        rs = slice(r * n, (r + 1) * n)
        o = of_ref[rs, :] + obw_ref[rs, :]
        ms = _seg_sum(o * o, ones_ref[...]) * (1.0 / 64)
        gate = g_ref[rs, :].astype(F32)
        oc = o * lax.rsqrt(ms + EPS) * go_ref[...] * (gate * jax.nn.sigmoid(gate))
        a0, a1 = V_WIDTH, V_WIDTH + GM_WIDTH
        m = (jnp.dot(oa_ref[rs, :], wo_ref[0:a0, :], preferred_element_type=F32)
             + jnp.dot(ob_ref[rs, :], wo_ref[a0:a1, :], preferred_element_type=F32)
             + jnp.dot(oc.astype(BF16), wo_ref[a1:, :], preferred_element_type=F32))
        x1 = x_ref[rs, :] + _rms(m, gpm_ref[...])
        h = _rms(x1, gpf_ref[...]).astype(BF16)
        acc = jnp.zeros(x1.shape, F32)
        for c in range(D_FF // FF_BLOCK):
            cs = slice(c * FF_BLOCK, (c + 1) * FF_BLOCK)
            a = jnp.maximum(jnp.dot(h, w1_ref[:, cs], preferred_element_type=F32), 0.0)
            acc = acc + jnp.dot((a * a).astype(BF16), w2_ref[cs, :], preferred_element_type=F32)
        y_ref[rs, :] = x1 + _rms(acc, gpo_ref[...])


def _out_ffn(x2d, oa, ob, of, obw, hqig, wl, ones_bd, *, tm):
    T = x2d.shape[0]
    row = lambda i: (i, 0)
    const = lambda a: pl.BlockSpec(a.shape, lambda i: (0, 0), pipeline_mode=pl.Buffered(1))
    ws = [wl['hg_g_o'], ones_bd, wl['w_out'], wl['g_post_mix'], wl['g_pre_ffn'],
          wl['w_ff1'], wl['w_ff2'], wl['g_post_ffn']]
    return pl.pallas_call(
        functools.partial(_out_ffn_kernel, nsub=max(1, tm // FFN_SUBTILE)),
        grid=(T // tm,),
        in_specs=[pl.BlockSpec((tm, D_MODEL), row), pl.BlockSpec((tm, V_WIDTH), row),
                  pl.BlockSpec((tm, GM_WIDTH), row), pl.BlockSpec((tm, HG_WIDTH), row),
                  pl.BlockSpec((tm, HG_WIDTH), row),
                  pl.BlockSpec((tm, HG_WIDTH), lambda i: (i, 2))] + [const(a) for a in ws],
        out_specs=pl.BlockSpec((tm, D_MODEL), row),
        out_shape=jax.ShapeDtypeStruct((T, D_MODEL), F32),
        compiler_params=pltpu.CompilerParams(dimension_semantics=("arbitrary",),
                                             vmem_limit_bytes=VMEM_LIMIT),
        name="out_ffn",
    )(x2d, oa, ob, of, obw, hqig, *ws)


def _rope_tables(seq):
    half = MLA_ROPE // 2
    inv = ROPE_BASE ** (-(jnp.arange(half, dtype=F32) / half))
    ang = jnp.arange(seq, dtype=F32)[:, None] * inv[None, :]
    cos, sin = jnp.cos(ang), jnp.sin(ang)
    pad = jnp.zeros((seq, HEAD_PAD - MLA_NOPE - MLA_ROPE), F32)
    ctab = jnp.concatenate([jnp.ones((seq, MLA_NOPE), F32), cos, cos, pad], axis=1)
    stab = jnp.concatenate([jnp.zeros((seq, MLA_NOPE), F32), -sin, sin, pad], axis=1)
    return ctab, stab


def _pack_weights(l, g_pre_mix, w_in, g_q_a, w_uq, g_kv_a, w_ukv, gm_ln_g, gm_ln_b, gm_w_s, gm_b_s,
                  hg_g_o, w_out, g_post_mix, g_pre_ffn, w_ff1, w_ff2, g_post_ffn):
    half = MLA_ROPE // 2
    swap = lambda a: jnp.concatenate([a[..., half:], a[..., :half]], axis=-1)
    row = lambda a: a[l][None, :]
    zpad = lambda n, k: jnp.zeros((n, k), F32)

    wi = w_in[l]
    o = np.cumsum((0, MLA_Q_RANK, MLA_KV_RANK, MLA_ROPE, GM_WIDTH, GM_WIDTH,
                   HG_KEYS, HG_KEYS, HG_KEYS, HG_WIDTH, HG_WIDTH))
    kr = wi[:, o[2]:o[3]]
    tail = HEAD_PAD - MLA_NOPE - MLA_ROPE
    kpe = jnp.concatenate([zpad(D_MODEL, MLA_NOPE), kr, zpad(D_MODEL, tail)], axis=1)
    kps = jnp.concatenate([zpad(D_MODEL, MLA_NOPE), swap(kr), zpad(D_MODEL, tail)], axis=1)
    w_in_p = jnp.concatenate([wi[:, o[0]:o[2]], kpe, kps, wi[:, o[3]:]], axis=1)

    wq = w_uq[l].reshape(MLA_Q_RANK, MLA_HEADS, MLA_NOPE + MLA_ROPE)
    qn, qr = wq[..., :MLA_NOPE], wq[..., MLA_NOPE:]
    zq = jnp.zeros((MLA_Q_RANK, MLA_HEADS, tail), F32)
    w_q = jnp.concatenate([qn, qr, zq], axis=-1).reshape(MLA_Q_RANK, QK_WIDTH)
    w_qs = jnp.concatenate([jnp.zeros_like(qn), swap(qr), zq], axis=-1).reshape(MLA_Q_RANK, QK_WIDTH)

    wkv = w_ukv[l].reshape(MLA_KV_RANK, MLA_HEADS, MLA_NOPE + MLA_V)
    kn, vv = wkv[..., :MLA_NOPE], wkv[..., MLA_NOPE:]
    zk = jnp.zeros((MLA_KV_RANK, MLA_HEADS, HEAD_PAD - MLA_NOPE), F32)
    w_k = jnp.concatenate([kn, zk], axis=-1).reshape(MLA_KV_RANK, QK_WIDTH)
    w_v = vv.reshape(MLA_KV_RANK, V_WIDTH)

    return dict(
        g_pre_mix=row(g_pre_mix), w_in=w_in_p.astype(BF16), g_q_a=row(g_q_a), w_q=w_q.astype(BF16),
        w_qs=w_qs.astype(BF16), g_kv_a=row(g_kv_a), w_k=w_k.astype(BF16), w_v=w_v.astype(BF16),
        gm_ln_g=row(gm_ln_g), gm_ln_b=row(gm_ln_b), gm_w_s=gm_w_s[l].astype(BF16),
        gm_b_s=jnp.broadcast_to(gm_b_s[l][:, None], (GM_CHUNK, GM_WIDTH)),
        hg_g_o=row(hg_g_o), w_out=w_out[l].astype(BF16), g_post_mix=row(g_post_mix),
        g_pre_ffn=row(g_pre_ffn), w_ff1=w_ff1[l].astype(BF16), w_ff2=w_ff2[l].astype(BF16),
        g_post_ffn=row(g_post_ffn))


def _scan_consts():
    k = np.arange(HG_KEYS)
    bd = (k[:, None] // HG_DK) == (k[None, :] // HG_DK)
    return jnp.asarray(bd, BF16), jnp.asarray(bd[:LANES, :LANES], F32)


def _trunk(x, layers, lbl, *, tm_in, tq, ck, nh, tm_out):
    batch, seq, _ = x.shape
    x2d = x.reshape(batch * seq, D_MODEL)
    tabs = _rope_tables(seq)
    consts = _scan_consts()
    ones_bd = consts[0]
    for l, wl in enumerate(layers):
        q, k, v, ob, hqig, logf, kk = _in_proj(x2d, wl, tabs, lbl, ones_bd, layer=l, seq=seq, tm=tm_in)
        oa, of, obw = _mixers(q, k, v, hqig, logf, kk, consts, batch=batch, seq=seq, tq=tq, ck=ck, nh=nh)
        x2d = _out_ffn(x2d, oa, ob, of, obw, hqig, wl, ones_bd, tm=tm_out)
    return x2d.reshape(batch, seq, D_MODEL)


def kernel(x_prompt, x_sample, hg_lb_logits, g_pre_mix, w_in, g_q_a, w_uq, g_kv_a, w_ukv, gm_ln_g, gm_ln_b,
           gm_w_s, gm_b_s, hg_g_o, w_out, g_post_mix, g_pre_ffn, w_ff1, w_ff2, g_post_ffn):
    depth = w_in.shape[0]
    layers = [_pack_weights(l, g_pre_mix, w_in, g_q_a, w_uq, g_kv_a, w_ukv, gm_ln_g, gm_ln_b, gm_w_s,
                            gm_b_s, hg_g_o, w_out, g_post_mix, g_pre_ffn, w_ff1, w_ff2, g_post_ffn)
              for l in range(depth)]
    lbl = hg_lb_logits.reshape(depth, 2 * HG_KEYS)
    y_prompt = _trunk(x_prompt, layers, lbl, tm_in=1024, tq=512, ck=256, nh=8, tm_out=1024)
    y_sample = _trunk(x_sample, layers, lbl, tm_in=1024, tq=256, ck=512, nh=8, tm_out=1024)
    return (y_prompt, y_sample)
```

```python
import functools

import numpy as np
import jax
import jax.numpy as jnp
from jax import lax
from jax.experimental import pallas as pl
from jax.experimental.pallas import tpu as pltpu

D_MODEL = 1024
MLA_HEADS = 8
MLA_NOPE = 64
MLA_ROPE = 32
MLA_V = 64
MLA_Q_RANK = D_MODEL // 4
MLA_KV_RANK = D_MODEL // 8
ROPE_BASE = 10000.0
GM_WIDTH = D_MODEL // 4
GM_GROUPS = 4
GM_CHUNK = 128
HG_HEADS = 4
HG_DK = 64
HG_KEYS = HG_HEADS * HG_DK
HG_WIDTH = HG_HEADS * 64
HG_CHUNK = 32
F_FLOOR = 1e-20
D_FF = 4 * D_MODEL
EPS = 1e-6

LANES = 128
HEAD_PAD = LANES
QK_WIDTH = MLA_HEADS * HEAD_PAD
V_WIDTH = MLA_HEADS * MLA_V
C_Q, C_KV, C_KPE, C_KPS, C_GU, C_GV, C_HQ, C_HF, C_HI, C_HG, C_END = (
    0, 256, 384, 512, 640, 896, 1152, 1408, 1920, 2176, 2432)
VMEM_LIMIT = 56 * 1024 * 1024
KV_DOUBLE_BUFFER_BUDGET = 40 * 1024 * 1024
NEG_BIG = -1e30
IN_SUBTILE = 512
LOG2E = 1.4426950408889634

F32 = jnp.float32
BF16 = jnp.bfloat16


def _rms(x, g):
    return x * lax.rsqrt(jnp.mean(x * x, axis=-1, keepdims=True) + EPS) * g


def _gelu(x):
    return 0.5 * x * (1.0 + lax.erf(x * (0.5 ** 0.5)))


def _seg_sum(x, ones_bd):
    hi = x.astype(BF16)
    lo = (x - hi.astype(F32)).astype(BF16)
    return (jnp.dot(hi, ones_bd, preferred_element_type=F32)
            + jnp.dot(lo, ones_bd, preferred_element_type=F32))


def _in_proj_kernel(x_ref, gpre_ref, win_ref, gq_ref, wq_ref, wqs_ref, gkv_ref, wk_ref, wv_ref,
                    ctab_ref, stab_ref, lng_ref, lnb_ref, ws_ref, bs_ref, lbl_ref, ones_ref,
                    q_out, k_out, v_out, ob_out, hqig_out, logf_out, kk_out, *, layer, nsub):
    scale = (MLA_NOPE + MLA_ROPE) ** -0.5 * LOG2E
    ones_bd = ones_ref[...]
    lg = lbl_ref[...]
    e = jnp.exp(lg - jnp.max(lg, axis=0, keepdims=True))
    p = e / jnp.sum(e, axis=0, keepdims=True)
    cum = p[0:1]
    for i in range(1, layer + 1):
        cum = cum + p[i:i + 1]
    lb = jnp.clip(cum - p[0:1], 0.0, 0.999)

    n = x_ref.shape[0] // nsub
    for r in range(nsub):
        rs = slice(r * n, (r + 1) * n)
        h = _rms(x_ref[rs, :], gpre_ref[...]).astype(BF16)
        z = jnp.dot(h, win_ref[...], preferred_element_type=F32)
        ctab = ctab_ref[rs, :]
        stab = stab_ref[rs, :]

        cqn = _rms(z[:, C_Q:C_KV], gq_ref[...]).astype(BF16)
        q = jnp.dot(cqn, wq_ref[...], preferred_element_type=F32)
        qs = jnp.dot(cqn, wqs_ref[...], preferred_element_type=F32)
        for hd in range(MLA_HEADS):
            sl = slice(hd * HEAD_PAD, (hd + 1) * HEAD_PAD)
            q_out[rs, sl] = ((q[:, sl] * ctab + qs[:, sl] * stab) * scale).astype(BF16)

        ckvn = _rms(z[:, C_KV:C_KPE], gkv_ref[...]).astype(BF16)
        kn = jnp.dot(ckvn, wk_ref[...], preferred_element_type=F32)
        kpe = z[:, C_KPE:C_KPS] * ctab + z[:, C_KPS:C_GU] * stab
        for hd in range(MLA_HEADS):
            sl = slice(hd * HEAD_PAD, (hd + 1) * HEAD_PAD)
            k_out[rs, sl] = (kn[:, sl] + kpe).astype(BF16)
        v_out[rs, :] = jnp.dot(ckvn, wv_ref[...], preferred_element_type=F32).astype(BF16)

        u = _gelu(z[:, C_GU:C_GV])
        v = _gelu(z[:, C_GV:C_HQ])
        inv_n = 1.0 / (GM_WIDTH // GM_GROUPS)
        xc = v - _seg_sum(v, ones_bd) * inv_n
        var = _seg_sum(xc * xc, ones_bd) * inv_n
        vn = (xc * lax.rsqrt(var + EPS) * lng_ref[...] + lnb_ref[...]).astype(BF16)
        for c in range(n // GM_CHUNK):
            cr = slice(c * GM_CHUNK, (c + 1) * GM_CHUNK)
            vs = jnp.dot(ws_ref[...], vn[cr], preferred_element_type=F32) + bs_ref[...]
            ob_out[r * n + c * GM_CHUNK:r * n + (c + 1) * GM_CHUNK, :] = (u[cr] * vs).astype(BF16)

        xf = z[:, C_HF:C_HI]
        f = lb + (1.0 - lb) * jax.nn.sigmoid(xf)
        logf_out[rs, :] = jnp.log(jnp.maximum(f, F_FLOOR)) * LOG2E
        kk_out[rs, :] = jnp.log((1.0 - lb) * jax.nn.sigmoid(-xf)) * LOG2E
        hqig_out[rs, 0:HG_KEYS] = z[:, C_HQ:C_HF].astype(BF16)
        hqig_out[rs, HG_KEYS:] = z[:, C_HI:C_END].astype(BF16)


def _in_proj(x2d, wl, tabs, lbl, ones_bd, *, layer, seq, tm):
    T = x2d.shape[0]
    nt = T // tm
    npos = seq // tm
    const = lambda i: (0, 0)
    row = lambda i: (i, 0)
    pos = lambda i: (i % npos, 0)
    full = lambda a: pl.BlockSpec(a.shape, const)
    ins = [
        (x2d, pl.BlockSpec((tm, D_MODEL), row)),
        (wl['g_pre_mix'], None), (wl['w_in'], None), (wl['g_q_a'], None), (wl['w_q'], None),
        (wl['w_qs'], None), (wl['g_kv_a'], None), (wl['w_k'], None), (wl['w_v'], None),
        (tabs[0], pl.BlockSpec((tm, HEAD_PAD), pos)), (tabs[1], pl.BlockSpec((tm, HEAD_PAD), pos)),
        (wl['gm_ln_g'], None), (wl['gm_ln_b'], None), (wl['gm_w_s'], None), (wl['gm_b_s'], None),
        (lbl, None), (ones_bd, None),
    ]
    in_specs = [s if s is not None else full(a) for a, s in ins]
    outs = [(QK_WIDTH, BF16), (QK_WIDTH, BF16), (V_WIDTH, BF16), (GM_WIDTH, BF16),
            (3 * HG_KEYS, BF16), (2 * HG_KEYS, F32), (2 * HG_KEYS, F32)]
    return pl.pallas_call(
        functools.partial(_in_proj_kernel, layer=layer, nsub=max(1, tm // IN_SUBTILE)),
        grid=(nt,),
        in_specs=in_specs,
        out_specs=[pl.BlockSpec((tm, w), row) for w, _ in outs],
        out_shape=[jax.ShapeDtypeStruct((T, w), dt) for w, dt in outs],
        compiler_params=pltpu.CompilerParams(dimension_semantics=("arbitrary",),
                                             vmem_limit_bytes=VMEM_LIMIT),
        name="in_proj",
    )(*[a for a, _ in ins])


def _attn_head(hd, q_ref, k_ref, v_ref, s_ref, ck, outs):
    tq, seq = q_ref.shape[0], k_ref.shape[0]
    nck = seq // ck
    sl = slice(hd * HEAD_PAD, (hd + 1) * HEAD_PAD)
    vsl = slice((hd // 2) * 2 * MLA_V, (hd // 2 + 1) * 2 * MLA_V)
    buf = hd % s_ref.shape[0]
    q = q_ref[:, sl]
    mp = None
    for c in range(nck):
        cs = slice(c * ck, (c + 1) * ck)
        s = lax.dot_general(q, k_ref[cs, sl], (((1,), (1,)), ((), ())),
                            preferred_element_type=F32)
        s_ref[buf, :, cs] = s
        for j in range(ck // LANES):
            t = s[:, j * LANES:(j + 1) * LANES]
            mp = t if mp is None else jnp.maximum(mp, t)
        yield
    m = jnp.max(mp, axis=-1, keepdims=True)
    ones = jnp.ones((ck, LANES), BF16)
    acc = jnp.zeros((tq, 2 * MLA_V + LANES), F32)
    for c in range(nck):
        cs = slice(c * ck, (c + 1) * ck)
        p = jnp.exp2(s_ref[buf, :, cs] - m).astype(BF16)
        acc = acc + jnp.dot(p, jnp.concatenate([v_ref[cs, vsl], ones], axis=1),
                            preferred_element_type=F32)
        yield
    outs.append(acc[:, :2 * MLA_V] / acc[:, 2 * MLA_V:])


SUB = 8
NSUB = HG_CHUNK // SUB
ROWS_PER_SUB = tuple(SUB * (NSUB - sb) for sb in range(NSUB))
LHS_ROWS = SUB * sum(ROWS_PER_SUB)


def _chunk_cumsum(x, reverse):
    n = x.shape[0]
    pos = lax.broadcasted_iota(jnp.int32, x.shape, 0) & (HG_CHUNK - 1)
    d = 1
    while d < HG_CHUNK:
        if reverse:
            shifted, ok = pltpu.roll(x, n - d, 0), pos < HG_CHUNK - d
        else:
            shifted, ok = pltpu.roll(x, d, 0), pos >= d
        x = x + jnp.where(ok, shifted, 0.0)
        d *= 2
    return x


def _hgrn_chunk(reverse, c, ones_ref, bmask_ref,
                o_ref, st_ref, qf_ref, vf_ref, b_ref, g_ref, lhs_ref, res_ref):
    ones_bd = ones_ref[...]
    bmask = bmask_ref[...]
    sub_iota = lax.broadcasted_iota(jnp.int32, (SUB, HG_KEYS), 0)
    rows = pl.ds(c * HG_CHUNK, HG_CHUNK)
    b = b_ref[rows, :]
    q = qf_ref[rows, :]
    g = g_ref[rows, :]
    v = vf_ref[rows, :]
    b_end = b[0:1] if reverse else b[HG_CHUNK - 1:HG_CHUNK]

    off = 0
    for sb in range(NSUB):
        if reverse:
            t0, t1 = 0, SUB * (sb + 1)
            diag = slice(t1 - SUB, t1)
        else:
            t0, t1 = SUB * sb, HG_CHUNK
            diag = slice(0, SUB)
        n = t1 - t0
        q_r = q[t0:t1]
        b_r = b[t0:t1]
        for si in range(SUB):
            s = SUB * sb + si
            d = b_r - g[s:s + 1]
            keep = (sub_iota <= si) if reverse else (sub_iota >= si)
            d_diag = jnp.where(keep, d[diag], NEG_BIG)
            if reverse:
                d = jnp.concatenate([d[:n - SUB], d_diag], axis=0) if n > SUB else d_diag
            else:
                d = jnp.concatenate([d_diag, d[SUB:]], axis=0) if n > SUB else d_diag
            lhs_ref[off:off + n, :] = q_r * jnp.exp2(d)
            off += n
            yield
    res_ref[...] = jnp.dot(lhs_ref[...].astype(BF16), ones_bd, preferred_element_type=F32)
    yield

    halves = [slice(pr * LANES, (pr + 1) * LANES) for pr in range(2)]
    st = [st_ref[pr] for pr in range(2)]
    qe = (q * jnp.exp2(b)).astype(BF16)
    o_inter = jnp.concatenate(
        [lax.dot_general(qe[:, hv], st[pr].astype(BF16), (((1,), (1,)), ((), ())),
                         preferred_element_type=F32) for pr, hv in enumerate(halves)], axis=1)
    o_blk = [o_inter[SUB * tb:SUB * (tb + 1)] for tb in range(NSUB)]
    off = 0
    for sb in range(NSUB):
        t0 = 0 if reverse else SUB * sb
        n = SUB * (sb + 1) if reverse else HG_CHUNK - SUB * sb
        for si in range(SUB):
            s = SUB * sb + si
            contrib = res_ref[off:off + n, :] * v[s:s + 1]
            for j in range(n // SUB):
                tb = t0 // SUB + j
                o_blk[tb] = o_blk[tb] + contrib[SUB * j:SUB * (j + 1)]
            off += n
            yield
    o_ref[rows, :] = jnp.concatenate(o_blk, axis=0)

    kdec = jnp.exp2(b_end - g).astype(BF16)
    vb = v.astype(BF16)
    dec = jnp.exp2(b_end)
    for pr, hv in enumerate(halves):
        upd = lax.dot_general(vb[:, hv], kdec[:, hv], (((0,), (0,)), ((), ())),
                              preferred_element_type=F32)
        st_ref[pr] = st[pr] * dec[:, hv] + upd * bmask


SCAN_PIECES = 2 * HG_CHUNK + 1


def _interleave(main, n_main, others, n_other):
    done = 0
    for i, _ in enumerate(main):
        target = -(-(i + 1) * n_other // n_main)
        while done < target:
            for g in others:
                next(g, None)
            done += 1
    for g in others:
        for _ in g:
            pass

def _mix_kernel(q_ref, k_ref, v_ref,
                qf_in, vf_in, lff_in, kkf_in, qb_in, vb_in, lfb_in, kkb_in, ones_ref, bmask_ref,
                oa_out, of_out, ob_out,
                s_ref, st_ref, qf_ref, vf_ref, b_ref, g_ref, lhs_ref, res_ref, *, ck, nh, nq, nchunk):
    @pl.when(pl.program_id(1) * nq + pl.program_id(2) == 0)
    def _():
        st_ref[...] = jnp.zeros_like(st_ref)

    for d, (q_in, v_in, lf_in, lk_in) in enumerate(((qf_in, vf_in, lff_in, kkf_in),
                                                     (qb_in, vb_in, lfb_in, kkb_in))):
        qf_ref[d] = q_in[...].astype(F32)
        vf_ref[d] = v_in[...].astype(F32)
        b = _chunk_cumsum(lf_in[...], reverse=bool(d))
        b_ref[d] = b
        g_ref[d] = b - lk_in[...]

    def scan_stream(reverse):
        d = int(reverse)
        o_out = ob_out if reverse else of_out
        for ci in range(nchunk):
            yield from _hgrn_chunk(reverse, nchunk - 1 - ci if reverse else ci, ones_ref,
                                   bmask_ref, o_out, st_ref.at[d], qf_ref.at[d], vf_ref.at[d],
                                   b_ref.at[d], g_ref.at[d], lhs_ref.at[d], res_ref.at[d])

    def attn_stream(outs):
        grp = s_ref.shape[0]
        for h0 in range(0, nh, grp):
            res = [[] for _ in range(grp)]
            live = [_attn_head(h0 + i, q_ref, k_ref, v_ref, s_ref, ck, res[i]) for i in range(grp)]
            while live:
                for gen in list(live):
                    if next(gen, "done") == "done":
                        live.remove(gen)
                    else:
                        yield
            for r in res:
                outs.extend(r)

    outs = []
    n_attn = nh * 2 * (k_ref.shape[0] // ck)
    n_scan = nchunk * SCAN_PIECES
    _interleave(attn_stream(outs), n_attn, [scan_stream(False), scan_stream(True)], n_scan)
    lane = lax.broadcasted_iota(jnp.int32, outs[0].shape, 1)
    for pr in range(nh // 2):
        oa_out[:, pr * 2 * MLA_V:(pr + 1) * 2 * MLA_V] = jnp.where(
            lane < MLA_V, outs[2 * pr], outs[2 * pr + 1]).astype(BF16)


def _mixers(q, k, v, hqig, logf, kk, consts, *, batch, seq, tq, ck, nh, grp):
    T = q.shape[0]
    nq = seq // tq
    nsteps = (MLA_HEADS // nh) * nq
    tb = seq // nsteps
    fwd = lambda col: (lambda b, h, i: (b * nsteps + h * nq + i, col))
    bwd = lambda col: (lambda b, h, i: (b * nsteps + nsteps - 1 - (h * nq + i), col))
    blk = lambda m: pl.BlockSpec((tb, HG_KEYS), m)
    const = lambda a: pl.BlockSpec(a.shape, lambda b, h, i: (0, 0))
    ones_bd, bmask = consts
    kv_bytes = seq * nh * (HEAD_PAD + MLA_V) * 2
    s_bytes = grp * tq * seq * 4
    kv_mode = {} if s_bytes + 2 * kv_bytes <= KV_DOUBLE_BUFFER_BUDGET else dict(pipeline_mode=pl.Buffered(1))
    scratch = [pltpu.VMEM((grp, tq, seq), F32),
               pltpu.VMEM((2, HG_HEADS // 2, LANES, LANES), F32),
               pltpu.VMEM((2, tb, HG_KEYS), F32), pltpu.VMEM((2, tb, HG_KEYS), F32),
               pltpu.VMEM((2, tb, HG_KEYS), F32), pltpu.VMEM((2, tb, HG_KEYS), F32),
               pltpu.VMEM((2, LHS_ROWS, HG_KEYS), F32), pltpu.VMEM((2, LHS_ROWS, HG_KEYS), F32)]
    return pl.pallas_call(
        functools.partial(_mix_kernel, ck=ck, nh=nh, nq=nq, nchunk=tb // HG_CHUNK),
        grid=(batch, MLA_HEADS // nh, nq),
        in_specs=[
            pl.BlockSpec((tq, nh * HEAD_PAD), lambda b, h, i: (b * nq + i, h)),
            pl.BlockSpec((seq, nh * HEAD_PAD), lambda b, h, i: (b, h), **kv_mode),
            pl.BlockSpec((seq, nh * MLA_V), lambda b, h, i: (b, h), **kv_mode),
            blk(fwd(0)), blk(fwd(1)), blk(fwd(0)), blk(fwd(0)),
            blk(bwd(0)), blk(bwd(1)), blk(bwd(1)), blk(bwd(1)),
            const(ones_bd), const(bmask)],
        out_specs=[pl.BlockSpec((tq, nh * MLA_V), lambda b, h, i: (b * nq + i, h)),
                   blk(fwd(0)), blk(bwd(0))],
        out_shape=[jax.ShapeDtypeStruct((T, V_WIDTH), BF16),
                   jax.ShapeDtypeStruct((T, HG_WIDTH), F32), jax.ShapeDtypeStruct((T, HG_WIDTH), F32)],
        scratch_shapes=scratch,
        compiler_params=pltpu.CompilerParams(
            dimension_semantics=("arbitrary", "arbitrary", "arbitrary"),
            vmem_limit_bytes=VMEM_LIMIT),
        name="mixers",
    )(q, k, v, hqig, hqig, logf, kk, hqig, hqig, logf, kk, ones_bd, bmask)


FF_BLOCK = 1024
FFN_SUBTILE = 512


def _out_ffn_kernel(x_ref, oa_ref, ob_ref, of_ref, obw_ref, g_ref, go_ref, ones_ref,
                    wo_ref, gpm_ref, gpf_ref, w1_ref, w2_ref, gpo_ref, y_ref, *, nsub):
    n = x_ref.shape[0] // nsub
    for r in range(nsub):
        rs = slice(r * n, (r + 1) * n)
        o = of_ref[rs, :] + obw_ref[rs, :]
        ms = _seg_sum(o * o, ones_ref[...]) * (1.0 / 64)
        gate = g_ref[rs, :].astype(F32)
        oc = o * lax.rsqrt(ms + EPS) * go_ref[...] * (gate * jax.nn.sigmoid(gate))
        a0, a1 = V_WIDTH, V_WIDTH + GM_WIDTH
        m = (jnp.dot(oa_ref[rs, :], wo_ref[0:a0, :], preferred_element_type=F32)
             + jnp.dot(ob_ref[rs, :], wo_ref[a0:a1, :], preferred_element_type=F32)
             + jnp.dot(oc.astype(BF16), wo_ref[a1:, :], preferred_element_type=F32))
        x1 = x_ref[rs, :] + _rms(m, gpm_ref[...])
        h = _rms(x1, gpf_ref[...]).astype(BF16)
        acc = jnp.zeros(x1.shape, F32)
        for c in range(D_FF // FF_BLOCK):
            cs = slice(c * FF_BLOCK, (c + 1) * FF_BLOCK)
            a = jnp.maximum(jnp.dot(h, w1_ref[:, cs], preferred_element_type=F32), 0.0)
            acc = acc + jnp.dot((a * a).astype(BF16), w2_ref[cs, :], preferred_element_type=F32)
        y_ref[rs, :] = x1 + _rms(acc, gpo_ref[...])


def _out_ffn(x2d, oa, ob, of, obw, hqig, wl, ones_bd, *, tm):
    T = x2d.shape[0]
    row = lambda i: (i, 0)
    const = lambda a: pl.BlockSpec(a.shape, lambda i: (0, 0), pipeline_mode=pl.Buffered(1))
    ws = [wl['hg_g_o'], ones_bd, wl['w_out'], wl['g_post_mix'], wl['g_pre_ffn'],
          wl['w_ff1'], wl['w_ff2'], wl['g_post_ffn']]
    return pl.pallas_call(
        functools.partial(_out_ffn_kernel, nsub=max(1, tm // FFN_SUBTILE)),
        grid=(T // tm,),
        in_specs=[pl.BlockSpec((tm, D_MODEL), row), pl.BlockSpec((tm, V_WIDTH), row),
                  pl.BlockSpec((tm, GM_WIDTH), row), pl.BlockSpec((tm, HG_WIDTH), row),
                  pl.BlockSpec((tm, HG_WIDTH), row),
                  pl.BlockSpec((tm, HG_WIDTH), lambda i: (i, 2))] + [const(a) for a in ws],
        out_specs=pl.BlockSpec((tm, D_MODEL), row),
        out_shape=jax.ShapeDtypeStruct((T, D_MODEL), F32),
        compiler_params=pltpu.CompilerParams(dimension_semantics=("arbitrary",),
                                             vmem_limit_bytes=VMEM_LIMIT),
        name="out_ffn",
    )(x2d, oa, ob, of, obw, hqig, *ws)


def _rope_tables(seq):
    half = MLA_ROPE // 2
    inv = ROPE_BASE ** (-(jnp.arange(half, dtype=F32) / half))
    ang = jnp.arange(seq, dtype=F32)[:, None] * inv[None, :]
    cos, sin = jnp.cos(ang), jnp.sin(ang)
    pad = jnp.zeros((seq, HEAD_PAD - MLA_NOPE - MLA_ROPE), F32)
    ctab = jnp.concatenate([jnp.ones((seq, MLA_NOPE), F32), cos, cos, pad], axis=1)
    stab = jnp.concatenate([jnp.zeros((seq, MLA_NOPE), F32), -sin, sin, pad], axis=1)
    return ctab, stab


def _pack_weights(l, g_pre_mix, w_in, g_q_a, w_uq, g_kv_a, w_ukv, gm_ln_g, gm_ln_b, gm_w_s, gm_b_s,
                  hg_g_o, w_out, g_post_mix, g_pre_ffn, w_ff1, w_ff2, g_post_ffn):
    half = MLA_ROPE // 2
    swap = lambda a: jnp.concatenate([a[..., half:], a[..., :half]], axis=-1)
    row = lambda a: a[l][None, :]
    zpad = lambda n, k: jnp.zeros((n, k), F32)

    wi = w_in[l]
    o = np.cumsum((0, MLA_Q_RANK, MLA_KV_RANK, MLA_ROPE, GM_WIDTH, GM_WIDTH,
                   HG_KEYS, HG_KEYS, HG_KEYS, HG_WIDTH, HG_WIDTH))
    kr = wi[:, o[2]:o[3]]
    tail = HEAD_PAD - MLA_NOPE - MLA_ROPE
    kpe = jnp.concatenate([zpad(D_MODEL, MLA_NOPE), kr, zpad(D_MODEL, tail)], axis=1)
    kps = jnp.concatenate([zpad(D_MODEL, MLA_NOPE), swap(kr), zpad(D_MODEL, tail)], axis=1)
    w_in_p = jnp.concatenate([wi[:, o[0]:o[2]], kpe, kps, wi[:, o[3]:]], axis=1)

    wq = w_uq[l].reshape(MLA_Q_RANK, MLA_HEADS, MLA_NOPE + MLA_ROPE)
    qn, qr = wq[..., :MLA_NOPE], wq[..., MLA_NOPE:]
    zq = jnp.zeros((MLA_Q_RANK, MLA_HEADS, tail), F32)
    w_q = jnp.concatenate([qn, qr, zq], axis=-1).reshape(MLA_Q_RANK, QK_WIDTH)
    w_qs = jnp.concatenate([jnp.zeros_like(qn), swap(qr), zq], axis=-1).reshape(MLA_Q_RANK, QK_WIDTH)

    wkv = w_ukv[l].reshape(MLA_KV_RANK, MLA_HEADS, MLA_NOPE + MLA_V)
    kn, vv = wkv[..., :MLA_NOPE], wkv[..., MLA_NOPE:]
    zk = jnp.zeros((MLA_KV_RANK, MLA_HEADS, HEAD_PAD - MLA_NOPE), F32)
    w_k = jnp.concatenate([kn, zk], axis=-1).reshape(MLA_KV_RANK, QK_WIDTH)
    w_v = vv.reshape(MLA_KV_RANK, V_WIDTH)

    return dict(
        g_pre_mix=row(g_pre_mix), w_in=w_in_p.astype(BF16), g_q_a=row(g_q_a), w_q=w_q.astype(BF16),
        w_qs=w_qs.astype(BF16), g_kv_a=row(g_kv_a), w_k=w_k.astype(BF16), w_v=w_v.astype(BF16),
        gm_ln_g=row(gm_ln_g), gm_ln_b=row(gm_ln_b), gm_w_s=gm_w_s[l].astype(BF16),
        gm_b_s=jnp.broadcast_to(gm_b_s[l][:, None], (GM_CHUNK, GM_WIDTH)),
        hg_g_o=row(hg_g_o), w_out=w_out[l].astype(BF16), g_post_mix=row(g_post_mix),
        g_pre_ffn=row(g_pre_ffn), w_ff1=w_ff1[l].astype(BF16), w_ff2=w_ff2[l].astype(BF16),
        g_post_ffn=row(g_post_ffn))


def _scan_consts():
    k = np.arange(HG_KEYS)
    bd = (k[:, None] // HG_DK) == (k[None, :] // HG_DK)
    return jnp.asarray(bd, BF16), jnp.asarray(bd[:LANES, :LANES], F32)


def _trunk(x, layers, lbl, *, tm_in, tq, ck, nh, grp, tm_out):
    batch, seq, _ = x.shape
    x2d = x.reshape(batch * seq, D_MODEL)
    tabs = _rope_tables(seq)
    consts = _scan_consts()
    ones_bd = consts[0]
    for l, wl in enumerate(layers):
        q, k, v, ob, hqig, logf, kk = _in_proj(x2d, wl, tabs, lbl, ones_bd, layer=l, seq=seq, tm=tm_in)
        oa, of, obw = _mixers(q, k, v, hqig, logf, kk, consts, batch=batch, seq=seq, tq=tq, ck=ck, nh=nh,
                              grp=grp)
        x2d = _out_ffn(x2d, oa, ob, of, obw, hqig, wl, ones_bd, tm=tm_out)
    return x2d.reshape(batch, seq, D_MODEL)


def kernel(x_prompt, x_sample, hg_lb_logits, g_pre_mix, w_in, g_q_a, w_uq, g_kv_a, w_ukv, gm_ln_g, gm_ln_b,
           gm_w_s, gm_b_s, hg_g_o, w_out, g_post_mix, g_pre_ffn, w_ff1, w_ff2, g_post_ffn):
    depth = w_in.shape[0]
    layers = [_pack_weights(l, g_pre_mix, w_in, g_q_a, w_uq, g_kv_a, w_ukv, gm_ln_g, gm_ln_b, gm_w_s,
                            gm_b_s, hg_g_o, w_out, g_post_mix, g_pre_ffn, w_ff1, w_ff2, g_post_ffn)
              for l in range(depth)]
    lbl = hg_lb_logits.reshape(depth, 2 * HG_KEYS)
    y_prompt = _trunk(x_prompt, layers, lbl, tm_in=1024, tq=512, ck=256, nh=8, grp=4, tm_out=1024)
    y_sample = _trunk(x_sample, layers, lbl, tm_in=1024, tq=256, ck=512, nh=8, grp=2, tm_out=1024)
    return (y_prompt, y_sample)
```

```python
import functools
from typing import NamedTuple

import numpy as np
import jax
import jax.numpy as jnp
from jax import lax
from jax.experimental import pallas as pl
from jax.experimental.pallas import tpu as pltpu

D_MODEL = 1024
MLA_HEADS = 8
MLA_NOPE = 64
MLA_ROPE = 32
MLA_V = 64
MLA_Q_RANK = D_MODEL // 4
MLA_KV_RANK = D_MODEL // 8
ROPE_BASE = 10000.0
GM_WIDTH = D_MODEL // 4
GM_GROUPS = 4
GM_CHUNK = 128
HG_HEADS = 4
HG_DK = 64
HG_KEYS = HG_HEADS * HG_DK
HG_WIDTH = HG_HEADS * 64
HG_CHUNK = 32
F_FLOOR = 1e-20
D_FF = 4 * D_MODEL
EPS = 1e-6

LANES = 128
HEAD_PAD = LANES
QK_WIDTH = MLA_HEADS * HEAD_PAD
V_WIDTH = MLA_HEADS * MLA_V
C_Q, C_KV, C_KPE, C_KPS, C_GU, C_GV, C_HQ, C_HF, C_HI, C_HG, C_END = (
    0, 256, 384, 512, 640, 896, 1152, 1408, 1920, 2176, 2432)
VMEM_LIMIT = 56 * 1024 * 1024
KV_DOUBLE_BUFFER_BUDGET = 40 * 1024 * 1024
SCORE_SCRATCH_BUDGET = 16 * 1024 * 1024
SCORE_CHUNK_ELEMS = 128 * 1024
NEG_BIG = -1e30
IN_SUBTILE = 512
LOG2E = 1.4426950408889634

F32 = jnp.float32
BF16 = jnp.bfloat16


def _rms(x, g):
    return x * lax.rsqrt(jnp.mean(x * x, axis=-1, keepdims=True) + EPS) * g


def _gelu(x):
    return 0.5 * x * (1.0 + lax.erf(x * (0.5 ** 0.5)))


def _seg_sum(x, ones_bd):
    hi = x.astype(BF16)
    lo = (x - hi.astype(F32)).astype(BF16)
    return (jnp.dot(hi, ones_bd, preferred_element_type=F32)
            + jnp.dot(lo, ones_bd, preferred_element_type=F32))


def _in_proj_kernel(x_ref, gpre_ref, win_ref, gq_ref, wq_ref, wqs_ref, gkv_ref, wk_ref, wv_ref,
                    ctab_ref, stab_ref, lng_ref, lnb_ref, ws_ref, bs_ref, lbl_ref, ones_ref,
                    q_out, k_out, v_out, ob_out, hqig_out, logf_out, kk_out, *, layer, nsub):
    scale = (MLA_NOPE + MLA_ROPE) ** -0.5 * LOG2E
    ones_bd = ones_ref[...]
    lg = lbl_ref[...]
    e = jnp.exp(lg - jnp.max(lg, axis=0, keepdims=True))
    p = e / jnp.sum(e, axis=0, keepdims=True)
    cum = p[0:1]
    for i in range(1, layer + 1):
        cum = cum + p[i:i + 1]
    lb = jnp.clip(cum - p[0:1], 0.0, 0.999)

    n = x_ref.shape[0] // nsub
    for r in range(nsub):
        rs = slice(r * n, (r + 1) * n)
        h = _rms(x_ref[rs, :], gpre_ref[...]).astype(BF16)
        z = jnp.dot(h, win_ref[...], preferred_element_type=F32)
        ctab = ctab_ref[rs, :]
        stab = stab_ref[rs, :]

        cqn = _rms(z[:, C_Q:C_KV], gq_ref[...]).astype(BF16)
        q = jnp.dot(cqn, wq_ref[...], preferred_element_type=F32)
        qs = jnp.dot(cqn, wqs_ref[...], preferred_element_type=F32)
        for hd in range(MLA_HEADS):
            sl = slice(hd * HEAD_PAD, (hd + 1) * HEAD_PAD)
            q_out[rs, sl] = ((q[:, sl] * ctab + qs[:, sl] * stab) * scale).astype(BF16)

        ckvn = _rms(z[:, C_KV:C_KPE], gkv_ref[...]).astype(BF16)
        kn = jnp.dot(ckvn, wk_ref[...], preferred_element_type=F32)
        kpe = z[:, C_KPE:C_KPS] * ctab + z[:, C_KPS:C_GU] * stab
        for hd in range(MLA_HEADS):
            sl = slice(hd * HEAD_PAD, (hd + 1) * HEAD_PAD)
            k_out[rs, sl] = (kn[:, sl] + kpe).astype(BF16)
        v_out[rs, :] = jnp.dot(ckvn, wv_ref[...], preferred_element_type=F32).astype(BF16)

        u = _gelu(z[:, C_GU:C_GV])
        v = _gelu(z[:, C_GV:C_HQ])
        inv_n = 1.0 / (GM_WIDTH // GM_GROUPS)
        xc = v - _seg_sum(v, ones_bd) * inv_n
        var = _seg_sum(xc * xc, ones_bd) * inv_n
        vn = (xc * lax.rsqrt(var + EPS) * lng_ref[...] + lnb_ref[...]).astype(BF16)
        for c in range(n // GM_CHUNK):
            cr = slice(c * GM_CHUNK, (c + 1) * GM_CHUNK)
            vs = jnp.dot(ws_ref[...], vn[cr], preferred_element_type=F32) + bs_ref[...]
            ob_out[r * n + c * GM_CHUNK:r * n + (c + 1) * GM_CHUNK, :] = (u[cr] * vs).astype(BF16)

        xf = z[:, C_HF:C_HI]
        f = lb + (1.0 - lb) * jax.nn.sigmoid(xf)
        logf_out[rs, :] = jnp.log(jnp.maximum(f, F_FLOOR)) * LOG2E
        kk_out[rs, :] = jnp.log((1.0 - lb) * jax.nn.sigmoid(-xf)) * LOG2E
        hqig_out[rs, 0:HG_KEYS] = z[:, C_HQ:C_HF].astype(BF16)
        hqig_out[rs, HG_KEYS:] = z[:, C_HI:C_END].astype(BF16)


def _in_proj(x2d, wl, tabs, lbl, ones_bd, *, layer, seq, tm):
    T = x2d.shape[0]
    nt = T // tm
    npos = seq // tm
    const = lambda i: (0, 0)
    row = lambda i: (i, 0)
    pos = lambda i: (i % npos, 0)
    full = lambda a: pl.BlockSpec(a.shape, const)
    ins = [
        (x2d, pl.BlockSpec((tm, D_MODEL), row)),
        (wl['g_pre_mix'], None), (wl['w_in'], None), (wl['g_q_a'], None), (wl['w_q'], None),
        (wl['w_qs'], None), (wl['g_kv_a'], None), (wl['w_k'], None), (wl['w_v'], None),
        (tabs[0], pl.BlockSpec((tm, HEAD_PAD), pos)), (tabs[1], pl.BlockSpec((tm, HEAD_PAD), pos)),
        (wl['gm_ln_g'], None), (wl['gm_ln_b'], None), (wl['gm_w_s'], None), (wl['gm_b_s'], None),
        (lbl, None), (ones_bd, None),
    ]
    in_specs = [s if s is not None else full(a) for a, s in ins]
    outs = [(QK_WIDTH, BF16), (QK_WIDTH, BF16), (V_WIDTH, BF16), (GM_WIDTH, BF16),
            (3 * HG_KEYS, BF16), (2 * HG_KEYS, F32), (2 * HG_KEYS, F32)]
    return pl.pallas_call(
        functools.partial(_in_proj_kernel, layer=layer, nsub=max(1, tm // IN_SUBTILE)),
        grid=(nt,),
        in_specs=in_specs,
        out_specs=[pl.BlockSpec((tm, w), row) for w, _ in outs],
        out_shape=[jax.ShapeDtypeStruct((T, w), dt) for w, dt in outs],
        compiler_params=pltpu.CompilerParams(dimension_semantics=("arbitrary",),
                                             vmem_limit_bytes=VMEM_LIMIT),
        name="in_proj",
    )(*[a for a, _ in ins])


def _attn_head(hd, q_ref, k_ref, v_ref, s_ref, ck, outs):
    tq, seq = q_ref.shape[0], k_ref.shape[0]
    nck = seq // ck
    sl = slice(hd * HEAD_PAD, (hd + 1) * HEAD_PAD)
    vsl = slice((hd // 2) * 2 * MLA_V, (hd // 2 + 1) * 2 * MLA_V)
    buf = hd % s_ref.shape[0]
    q = q_ref[:, sl]
    mp = None
    for c in range(nck):
        cs = slice(c * ck, (c + 1) * ck)
        s = lax.dot_general(q, k_ref[cs, sl], (((1,), (1,)), ((), ())),
                            preferred_element_type=F32)
        s_ref[buf, :, cs] = s
        for j in range(ck // LANES):
            t = s[:, j * LANES:(j + 1) * LANES]
            mp = t if mp is None else jnp.maximum(mp, t)
        yield
    m = jnp.max(mp, axis=-1, keepdims=True)
    ones = jnp.ones((ck, LANES), BF16)
    acc = jnp.zeros((tq, 2 * MLA_V + LANES), F32)
    for c in range(nck):
        cs = slice(c * ck, (c + 1) * ck)
        p = jnp.exp2(s_ref[buf, :, cs] - m).astype(BF16)
        acc = acc + jnp.dot(p, jnp.concatenate([v_ref[cs, vsl], ones], axis=1),
                            preferred_element_type=F32)
        yield
    outs.append(acc[:, :2 * MLA_V] / acc[:, 2 * MLA_V:])


SUB = 8
NSUB = HG_CHUNK // SUB
ROWS_PER_SUB = tuple(SUB * (NSUB - sb) for sb in range(NSUB))
LHS_ROWS = SUB * sum(ROWS_PER_SUB)


def _chunk_cumsum(x, reverse):
    n = x.shape[0]
    pos = lax.broadcasted_iota(jnp.int32, x.shape, 0) & (HG_CHUNK - 1)
    d = 1
    while d < HG_CHUNK:
        if reverse:
            shifted, ok = pltpu.roll(x, n - d, 0), pos < HG_CHUNK - d
        else:
            shifted, ok = pltpu.roll(x, d, 0), pos >= d
        x = x + jnp.where(ok, shifted, 0.0)
        d *= 2
    return x


def _hgrn_chunk(reverse, c, ones_ref, bmask_ref,
                o_ref, st_ref, qf_ref, vf_ref, b_ref, g_ref, lhs_ref, res_ref):
    ones_bd = ones_ref[...]
    bmask = bmask_ref[...]
    sub_iota = lax.broadcasted_iota(jnp.int32, (SUB, HG_KEYS), 0)
    rows = pl.ds(c * HG_CHUNK, HG_CHUNK)
    b = b_ref[rows, :]
    q = qf_ref[rows, :]
    g = g_ref[rows, :]
    v = vf_ref[rows, :]
    b_end = b[0:1] if reverse else b[HG_CHUNK - 1:HG_CHUNK]

    off = 0
    for sb in range(NSUB):
        if reverse:
            t0, t1 = 0, SUB * (sb + 1)
            diag = slice(t1 - SUB, t1)
        else:
            t0, t1 = SUB * sb, HG_CHUNK
            diag = slice(0, SUB)
        n = t1 - t0
        q_r = q[t0:t1]
        b_r = b[t0:t1]
        for si in range(SUB):
            s = SUB * sb + si
            d = b_r - g[s:s + 1]
            keep = (sub_iota <= si) if reverse else (sub_iota >= si)
            d_diag = jnp.where(keep, d[diag], NEG_BIG)
            if reverse:
                d = jnp.concatenate([d[:n - SUB], d_diag], axis=0) if n > SUB else d_diag
            else:
                d = jnp.concatenate([d_diag, d[SUB:]], axis=0) if n > SUB else d_diag
            lhs_ref[off:off + n, :] = q_r * jnp.exp2(d)
            off += n
            yield
    res_ref[...] = jnp.dot(lhs_ref[...].astype(BF16), ones_bd, preferred_element_type=F32)
    yield

    halves = [slice(pr * LANES, (pr + 1) * LANES) for pr in range(2)]
    st = [st_ref[pr] for pr in range(2)]
    qe = (q * jnp.exp2(b)).astype(BF16)
    o_inter = jnp.concatenate(
        [lax.dot_general(qe[:, hv], st[pr].astype(BF16), (((1,), (1,)), ((), ())),
                         preferred_element_type=F32) for pr, hv in enumerate(halves)], axis=1)
    o_blk = [o_inter[SUB * tb:SUB * (tb + 1)] for tb in range(NSUB)]
    off = 0
    for sb in range(NSUB):
        t0 = 0 if reverse else SUB * sb
        n = SUB * (sb + 1) if reverse else HG_CHUNK - SUB * sb
        for si in range(SUB):
            s = SUB * sb + si
            contrib = res_ref[off:off + n, :] * v[s:s + 1]
            for j in range(n // SUB):
                tb = t0 // SUB + j
                o_blk[tb] = o_blk[tb] + contrib[SUB * j:SUB * (j + 1)]
            off += n
            yield
    o_ref[rows, :] = jnp.concatenate(o_blk, axis=0)

    kdec = jnp.exp2(b_end - g).astype(BF16)
    vb = v.astype(BF16)
    dec = jnp.exp2(b_end)
    for pr, hv in enumerate(halves):
        upd = lax.dot_general(vb[:, hv], kdec[:, hv], (((0,), (0,)), ((), ())),
                              preferred_element_type=F32)
        st_ref[pr] = st[pr] * dec[:, hv] + upd * bmask


SCAN_PIECES = 2 * HG_CHUNK + 1


def _interleave(main, n_main, others, n_other):
    done = 0
    for i, _ in enumerate(main):
        target = -(-(i + 1) * n_other // n_main)
        while done < target:
            for g in others:
                next(g, None)
            done += 1
    for g in others:
        for _ in g:
            pass

def _mix_kernel(q_ref, k_ref, v_ref,
                qf_in, vf_in, lff_in, kkf_in, qb_in, vb_in, lfb_in, kkb_in, ones_ref, bmask_ref,
                oa_out, of_out, ob_out,
                s_ref, st_ref, qf_ref, vf_ref, b_ref, g_ref, lhs_ref, res_ref, *, ck, nh, nq, nchunk):
    @pl.when(pl.program_id(1) * nq + pl.program_id(2) == 0)
    def _():
        st_ref[...] = jnp.zeros_like(st_ref)

    for d, (q_in, v_in, lf_in, lk_in) in enumerate(((qf_in, vf_in, lff_in, kkf_in),
                                                     (qb_in, vb_in, lfb_in, kkb_in))):
        qf_ref[d] = q_in[...].astype(F32)
        vf_ref[d] = v_in[...].astype(F32)
        b = _chunk_cumsum(lf_in[...], reverse=bool(d))
        b_ref[d] = b
        g_ref[d] = b - lk_in[...]

    def scan_stream(reverse):
        d = int(reverse)
        o_out = ob_out if reverse else of_out
        for ci in range(nchunk):
            yield from _hgrn_chunk(reverse, nchunk - 1 - ci if reverse else ci, ones_ref,
                                   bmask_ref, o_out, st_ref.at[d], qf_ref.at[d], vf_ref.at[d],
                                   b_ref.at[d], g_ref.at[d], lhs_ref.at[d], res_ref.at[d])

    def attn_stream(outs):
        grp = s_ref.shape[0]
        for h0 in range(0, nh, grp):
            res = [[] for _ in range(grp)]
            live = [_attn_head(h0 + i, q_ref, k_ref, v_ref, s_ref, ck, res[i]) for i in range(grp)]
            while live:
                for gen in list(live):
                    if next(gen, "done") == "done":
                        live.remove(gen)
                    else:
                        yield
            for r in res:
                outs.extend(r)

    outs = []
    n_attn = nh * 2 * (k_ref.shape[0] // ck)
    n_scan = nchunk * SCAN_PIECES
    _interleave(attn_stream(outs), n_attn, [scan_stream(False), scan_stream(True)], n_scan)
    lane = lax.broadcasted_iota(jnp.int32, outs[0].shape, 1)
    for pr in range(nh // 2):
        oa_out[:, pr * 2 * MLA_V:(pr + 1) * 2 * MLA_V] = jnp.where(
            lane < MLA_V, outs[2 * pr], outs[2 * pr + 1]).astype(BF16)


def _mixers(q, k, v, hqig, logf, kk, consts, *, batch, seq, tq, ck, nh, grp):
    T = q.shape[0]
    nq = seq // tq
    nsteps = (MLA_HEADS // nh) * nq
    tb = seq // nsteps
    fwd = lambda col: (lambda b, h, i: (b * nsteps + h * nq + i, col))
    bwd = lambda col: (lambda b, h, i: (b * nsteps + nsteps - 1 - (h * nq + i), col))
    blk = lambda m: pl.BlockSpec((tb, HG_KEYS), m)
    const = lambda a: pl.BlockSpec(a.shape, lambda b, h, i: (0, 0))
    ones_bd, bmask = consts
    kv_bytes = seq * nh * (HEAD_PAD + MLA_V) * 2
    s_bytes = grp * tq * seq * 4
    kv_mode = {} if s_bytes + 2 * kv_bytes <= KV_DOUBLE_BUFFER_BUDGET else dict(pipeline_mode=pl.Buffered(1))
    scratch = [pltpu.VMEM((grp, tq, seq), F32),
               pltpu.VMEM((2, HG_HEADS // 2, LANES, LANES), F32),
               pltpu.VMEM((2, tb, HG_KEYS), F32), pltpu.VMEM((2, tb, HG_KEYS), F32),
               pltpu.VMEM((2, tb, HG_KEYS), F32), pltpu.VMEM((2, tb, HG_KEYS), F32),
               pltpu.VMEM((2, LHS_ROWS, HG_KEYS), F32), pltpu.VMEM((2, LHS_ROWS, HG_KEYS), F32)]
    return pl.pallas_call(
        functools.partial(_mix_kernel, ck=ck, nh=nh, nq=nq, nchunk=tb // HG_CHUNK),
        grid=(batch, MLA_HEADS // nh, nq),
        in_specs=[
            pl.BlockSpec((tq, nh * HEAD_PAD), lambda b, h, i: (b * nq + i, h)),
            pl.BlockSpec((seq, nh * HEAD_PAD), lambda b, h, i: (b, h), **kv_mode),
            pl.BlockSpec((seq, nh * MLA_V), lambda b, h, i: (b, h), **kv_mode),
            blk(fwd(0)), blk(fwd(1)), blk(fwd(0)), blk(fwd(0)),
            blk(bwd(0)), blk(bwd(1)), blk(bwd(1)), blk(bwd(1)),
            const(ones_bd), const(bmask)],
        out_specs=[pl.BlockSpec((tq, nh * MLA_V), lambda b, h, i: (b * nq + i, h)),
                   blk(fwd(0)), blk(bwd(0))],
        out_shape=[jax.ShapeDtypeStruct((T, V_WIDTH), BF16),
                   jax.ShapeDtypeStruct((T, HG_WIDTH), F32), jax.ShapeDtypeStruct((T, HG_WIDTH), F32)],
        scratch_shapes=scratch,
        compiler_params=pltpu.CompilerParams(
            dimension_semantics=("arbitrary", "arbitrary", "arbitrary"),
            vmem_limit_bytes=VMEM_LIMIT),
        name="mixers",
    )(q, k, v, hqig, hqig, logf, kk, hqig, hqig, logf, kk, ones_bd, bmask)


FF_BLOCK = 1024
FFN_SUBTILE = 512


def _out_ffn_kernel(x_ref, oa_ref, ob_ref, of_ref, obw_ref, g_ref, go_ref, ones_ref,
                    wo_ref, gpm_ref, gpf_ref, w1_ref, w2_ref, gpo_ref, y_ref, *, nsub):
    n = x_ref.shape[0] // nsub
    for r in range(nsub):
        rs = slice(r * n, (r + 1) * n)
        o = of_ref[rs, :] + obw_ref[rs, :]
        ms = _seg_sum(o * o, ones_ref[...]) * (1.0 / 64)
        gate = g_ref[rs, :].astype(F32)
        oc = o * lax.rsqrt(ms + EPS) * go_ref[...] * (gate * jax.nn.sigmoid(gate))
        a0, a1 = V_WIDTH, V_WIDTH + GM_WIDTH
        m = (jnp.dot(oa_ref[rs, :], wo_ref[0:a0, :], preferred_element_type=F32)
             + jnp.dot(ob_ref[rs, :], wo_ref[a0:a1, :], preferred_element_type=F32)
             + jnp.dot(oc.astype(BF16), wo_ref[a1:, :], preferred_element_type=F32))
        x1 = x_ref[rs, :] + _rms(m, gpm_ref[...])
        h = _rms(x1, gpf_ref[...]).astype(BF16)
        acc = jnp.zeros(x1.shape, F32)
        for c in range(D_FF // FF_BLOCK):
            cs = slice(c * FF_BLOCK, (c + 1) * FF_BLOCK)
            a = jnp.maximum(jnp.dot(h, w1_ref[:, cs], preferred_element_type=F32), 0.0)
            acc = acc + jnp.dot((a * a).astype(BF16), w2_ref[cs, :], preferred_element_type=F32)
        y_ref[rs, :] = x1 + _rms(acc, gpo_ref[...])


def _out_ffn(x2d, oa, ob, of, obw, hqig, wl, ones_bd, *, tm):
    T = x2d.shape[0]
    row = lambda i: (i, 0)
    const = lambda a: pl.BlockSpec(a.shape, lambda i: (0, 0), pipeline_mode=pl.Buffered(1))
    ws = [wl['hg_g_o'], ones_bd, wl['w_out'], wl['g_post_mix'], wl['g_pre_ffn'],
          wl['w_ff1'], wl['w_ff2'], wl['g_post_ffn']]
    return pl.pallas_call(
        functools.partial(_out_ffn_kernel, nsub=max(1, tm // FFN_SUBTILE)),
        grid=(T // tm,),
        in_specs=[pl.BlockSpec((tm, D_MODEL), row), pl.BlockSpec((tm, V_WIDTH), row),
                  pl.BlockSpec((tm, GM_WIDTH), row), pl.BlockSpec((tm, HG_WIDTH), row),
                  pl.BlockSpec((tm, HG_WIDTH), row),
                  pl.BlockSpec((tm, HG_WIDTH), lambda i: (i, 2))] + [const(a) for a in ws],
        out_specs=pl.BlockSpec((tm, D_MODEL), row),
        out_shape=jax.ShapeDtypeStruct((T, D_MODEL), F32),
        compiler_params=pltpu.CompilerParams(dimension_semantics=("arbitrary",),
                                             vmem_limit_bytes=VMEM_LIMIT),
        name="out_ffn",
    )(x2d, oa, ob, of, obw, hqig, *ws)


def _rope_tables(seq):
    half = MLA_ROPE // 2
    inv = ROPE_BASE ** (-(jnp.arange(half, dtype=F32) / half))
    ang = jnp.arange(seq, dtype=F32)[:, None] * inv[None, :]
    cos, sin = jnp.cos(ang), jnp.sin(ang)
    pad = jnp.zeros((seq, HEAD_PAD - MLA_NOPE - MLA_ROPE), F32)
    ctab = jnp.concatenate([jnp.ones((seq, MLA_NOPE), F32), cos, cos, pad], axis=1)
    stab = jnp.concatenate([jnp.zeros((seq, MLA_NOPE), F32), -sin, sin, pad], axis=1)
    return ctab, stab


def _pack_weights(l, g_pre_mix, w_in, g_q_a, w_uq, g_kv_a, w_ukv, gm_ln_g, gm_ln_b, gm_w_s, gm_b_s,
                  hg_g_o, w_out, g_post_mix, g_pre_ffn, w_ff1, w_ff2, g_post_ffn):
    half = MLA_ROPE // 2
    swap = lambda a: jnp.concatenate([a[..., half:], a[..., :half]], axis=-1)
    row = lambda a: a[l][None, :]
    zpad = lambda n, k: jnp.zeros((n, k), F32)

    wi = w_in[l]
    o = np.cumsum((0, MLA_Q_RANK, MLA_KV_RANK, MLA_ROPE, GM_WIDTH, GM_WIDTH,
                   HG_KEYS, HG_KEYS, HG_KEYS, HG_WIDTH, HG_WIDTH))
    kr = wi[:, o[2]:o[3]]
    tail = HEAD_PAD - MLA_NOPE - MLA_ROPE
    kpe = jnp.concatenate([zpad(D_MODEL, MLA_NOPE), kr, zpad(D_MODEL, tail)], axis=1)
    kps = jnp.concatenate([zpad(D_MODEL, MLA_NOPE), swap(kr), zpad(D_MODEL, tail)], axis=1)
    w_in_p = jnp.concatenate([wi[:, o[0]:o[2]], kpe, kps, wi[:, o[3]:]], axis=1)

    wq = w_uq[l].reshape(MLA_Q_RANK, MLA_HEADS, MLA_NOPE + MLA_ROPE)
    qn, qr = wq[..., :MLA_NOPE], wq[..., MLA_NOPE:]
    zq = jnp.zeros((MLA_Q_RANK, MLA_HEADS, tail), F32)
    w_q = jnp.concatenate([qn, qr, zq], axis=-1).reshape(MLA_Q_RANK, QK_WIDTH)
    w_qs = jnp.concatenate([jnp.zeros_like(qn), swap(qr), zq], axis=-1).reshape(MLA_Q_RANK, QK_WIDTH)

    wkv = w_ukv[l].reshape(MLA_KV_RANK, MLA_HEADS, MLA_NOPE + MLA_V)
    kn, vv = wkv[..., :MLA_NOPE], wkv[..., MLA_NOPE:]
    zk = jnp.zeros((MLA_KV_RANK, MLA_HEADS, HEAD_PAD - MLA_NOPE), F32)
    w_k = jnp.concatenate([kn, zk], axis=-1).reshape(MLA_KV_RANK, QK_WIDTH)
    w_v = vv.reshape(MLA_KV_RANK, V_WIDTH)

    return dict(
        g_pre_mix=row(g_pre_mix), w_in=w_in_p.astype(BF16), g_q_a=row(g_q_a), w_q=w_q.astype(BF16),
        w_qs=w_qs.astype(BF16), g_kv_a=row(g_kv_a), w_k=w_k.astype(BF16), w_v=w_v.astype(BF16),
        gm_ln_g=row(gm_ln_g), gm_ln_b=row(gm_ln_b), gm_w_s=gm_w_s[l].astype(BF16),
        gm_b_s=jnp.broadcast_to(gm_b_s[l][:, None], (GM_CHUNK, GM_WIDTH)),
        hg_g_o=row(hg_g_o), w_out=w_out[l].astype(BF16), g_post_mix=row(g_post_mix),
        g_pre_ffn=row(g_pre_ffn), w_ff1=w_ff1[l].astype(BF16), w_ff2=w_ff2[l].astype(BF16),
        g_post_ffn=row(g_post_ffn))


def _scan_consts():
    k = np.arange(HG_KEYS)
    bd = (k[:, None] // HG_DK) == (k[None, :] // HG_DK)
    return jnp.asarray(bd, BF16), jnp.asarray(bd[:LANES, :LANES], F32)


class Tiles(NamedTuple):
    tm: int
    tq: int
    ck: int
    nh: int
    grp: int


def _tiles(seq):
    per_row = seq * 4
    tq = next(t for t in (512, 256, 128) if 2 * t * per_row <= SCORE_SCRATCH_BUDGET)
    grp = min(4, SCORE_SCRATCH_BUDGET // (tq * per_row))
    ck = SCORE_CHUNK_ELEMS // tq
    return Tiles(tm=2 * IN_SUBTILE, tq=tq, ck=ck, nh=MLA_HEADS, grp=grp)


def _trunk(x, layers, lbl, tiles=None):
    batch, seq, _ = x.shape
    t = tiles or _tiles(seq)
    x2d = x.reshape(batch * seq, D_MODEL)
    tabs = _rope_tables(seq)
    consts = _scan_consts()
    ones_bd = consts[0]
    for l, wl in enumerate(layers):
        q, k, v, ob, hqig, logf, kk = _in_proj(x2d, wl, tabs, lbl, ones_bd, layer=l, seq=seq, tm=t.tm)
        oa, of, obw = _mixers(q, k, v, hqig, logf, kk, consts, batch=batch, seq=seq, tq=t.tq, ck=t.ck,
                              nh=t.nh, grp=t.grp)
        x2d = _out_ffn(x2d, oa, ob, of, obw, hqig, wl, ones_bd, tm=t.tm)
    return x2d.reshape(batch, seq, D_MODEL)


def kernel(x_prompt, x_sample, hg_lb_logits, g_pre_mix, w_in, g_q_a, w_uq, g_kv_a, w_ukv, gm_ln_g, gm_ln_b,
           gm_w_s, gm_b_s, hg_g_o, w_out, g_post_mix, g_pre_ffn, w_ff1, w_ff2, g_post_ffn):
    depth = w_in.shape[0]
    layers = [_pack_weights(l, g_pre_mix, w_in, g_q_a, w_uq, g_kv_a, w_ukv, gm_ln_g, gm_ln_b, gm_w_s,
                            gm_b_s, hg_g_o, w_out, g_post_mix, g_pre_ffn, w_ff1, w_ff2, g_post_ffn)
              for l in range(depth)]
    lbl = hg_lb_logits.reshape(depth, 2 * HG_KEYS)
    return (_trunk(x_prompt, layers, lbl), _trunk(x_sample, layers, lbl))
```

```python
import functools
from typing import NamedTuple

import numpy as np
import jax
import jax.numpy as jnp
from jax import lax
from jax.experimental import pallas as pl
from jax.experimental.pallas import tpu as pltpu

D_MODEL = 1024
MLA_HEADS = 8
MLA_NOPE = 64
MLA_ROPE = 32
MLA_V = 64
MLA_Q_RANK = D_MODEL // 4
MLA_KV_RANK = D_MODEL // 8
ROPE_BASE = 10000.0
GM_WIDTH = D_MODEL // 4
GM_GROUPS = 4
GM_CHUNK = 128
HG_HEADS = 4
HG_DK = 64
HG_KEYS = HG_HEADS * HG_DK
HG_WIDTH = HG_HEADS * 64
HG_CHUNK = 32
F_FLOOR = 1e-20
D_FF = 4 * D_MODEL
EPS = 1e-6

LANES = 128
HEAD_PAD = LANES
QK_WIDTH = MLA_HEADS * HEAD_PAD
V_WIDTH = MLA_HEADS * MLA_V
C_Q, C_KV, C_KPE, C_KPS, C_GU, C_GV, C_HQ, C_HF, C_HI, C_HG, C_END = (
    0, 256, 384, 512, 640, 896, 1152, 1408, 1920, 2176, 2432)
VMEM_LIMIT = 56 * 1024 * 1024
KV_DOUBLE_BUFFER_BUDGET = 40 * 1024 * 1024
SCORE_SCRATCH_BUDGET = 16 * 1024 * 1024
SCORE_CHUNK_ELEMS = 128 * 1024
NEG_BIG = -1e30
IN_SUBTILE = 512
LOG2E = 1.4426950408889634

F32 = jnp.float32
BF16 = jnp.bfloat16


def _rms(x, g):
    return x * lax.rsqrt(jnp.mean(x * x, axis=-1, keepdims=True) + EPS) * g


def _gelu(x):
    return 0.5 * x * (1.0 + lax.erf(x * (0.5 ** 0.5)))


def _seg_sum(x, ones_bd):
    hi = x.astype(BF16)
    lo = (x - hi.astype(F32)).astype(BF16)
    return (jnp.dot(hi, ones_bd, preferred_element_type=F32)
            + jnp.dot(lo, ones_bd, preferred_element_type=F32))


def _in_proj_kernel(x_ref, gpre_ref, win_ref, gq_ref, wq_ref, wqs_ref, gkv_ref, wk_ref, wv_ref,
                    ctab_ref, stab_ref, lng_ref, lnb_ref, ws_ref, bs_ref, lbl_ref, ones_ref,
                    q_out, k_out, v_out, ob_out, hqig_out, logf_out, kk_out, *, layer, nsub):
    scale = (MLA_NOPE + MLA_ROPE) ** -0.5 * LOG2E
    ones_bd = ones_ref[...]
    lg = lbl_ref[...]
    e = jnp.exp(lg - jnp.max(lg, axis=0, keepdims=True))
    p = e / jnp.sum(e, axis=0, keepdims=True)
    cum = p[0:1]
    for i in range(1, layer + 1):
        cum = cum + p[i:i + 1]
    lb = jnp.clip(cum - p[0:1], 0.0, 0.999)

    n = x_ref.shape[0] // nsub
    for r in range(nsub):
        rs = slice(r * n, (r + 1) * n)
        h = _rms(x_ref[rs, :], gpre_ref[...]).astype(BF16)
        z = jnp.dot(h, win_ref[...], preferred_element_type=F32)
        ctab = ctab_ref[rs, :]
        stab = stab_ref[rs, :]

        cqn = _rms(z[:, C_Q:C_KV], gq_ref[...]).astype(BF16)
        q = jnp.dot(cqn, wq_ref[...], preferred_element_type=F32)
        qs = jnp.dot(cqn, wqs_ref[...], preferred_element_type=F32)
        for hd in range(MLA_HEADS):
            sl = slice(hd * HEAD_PAD, (hd + 1) * HEAD_PAD)
            q_out[rs, sl] = ((q[:, sl] * ctab + qs[:, sl] * stab) * scale).astype(BF16)

        ckvn = _rms(z[:, C_KV:C_KPE], gkv_ref[...]).astype(BF16)
        kn = jnp.dot(ckvn, wk_ref[...], preferred_element_type=F32)
        kpe = z[:, C_KPE:C_KPS] * ctab + z[:, C_KPS:C_GU] * stab
        for hd in range(MLA_HEADS):
            sl = slice(hd * HEAD_PAD, (hd + 1) * HEAD_PAD)
            k_out[rs, sl] = (kn[:, sl] + kpe).astype(BF16)
        v_out[rs, :] = jnp.dot(ckvn, wv_ref[...], preferred_element_type=F32).astype(BF16)

        u = _gelu(z[:, C_GU:C_GV])
        v = _gelu(z[:, C_GV:C_HQ])
        inv_n = 1.0 / (GM_WIDTH // GM_GROUPS)
        xc = v - _seg_sum(v, ones_bd) * inv_n
        var = _seg_sum(xc * xc, ones_bd) * inv_n
        vn = (xc * lax.rsqrt(var + EPS) * lng_ref[...] + lnb_ref[...]).astype(BF16)
        for c in range(n // GM_CHUNK):
            cr = slice(c * GM_CHUNK, (c + 1) * GM_CHUNK)
            vs = jnp.dot(ws_ref[...], vn[cr], preferred_element_type=F32) + bs_ref[...]
            ob_out[r * n + c * GM_CHUNK:r * n + (c + 1) * GM_CHUNK, :] = (u[cr] * vs).astype(BF16)

        xf = z[:, C_HF:C_HI]
        f = lb + (1.0 - lb) * jax.nn.sigmoid(xf)
        logf_out[rs, :] = jnp.log(jnp.maximum(f, F_FLOOR)) * LOG2E
        kk_out[rs, :] = jnp.log((1.0 - lb) * jax.nn.sigmoid(-xf)) * LOG2E
        hqig_out[rs, 0:HG_KEYS] = z[:, C_HQ:C_HF].astype(BF16)
        hqig_out[rs, HG_KEYS:] = z[:, C_HI:C_END].astype(BF16)


def _in_proj(x2d, wl, tabs, lbl, ones_bd, *, layer, seq, tm):
    T = x2d.shape[0]
    nt = T // tm
    npos = seq // tm
    const = lambda i: (0, 0)
    row = lambda i: (i, 0)
    pos = lambda i: (i % npos, 0)
    full = lambda a: pl.BlockSpec(a.shape, const)
    ins = [
        (x2d, pl.BlockSpec((tm, D_MODEL), row)),
        (wl['g_pre_mix'], None), (wl['w_in'], None), (wl['g_q_a'], None), (wl['w_q'], None),
        (wl['w_qs'], None), (wl['g_kv_a'], None), (wl['w_k'], None), (wl['w_v'], None),
        (tabs[0], pl.BlockSpec((tm, HEAD_PAD), pos)), (tabs[1], pl.BlockSpec((tm, HEAD_PAD), pos)),
        (wl['gm_ln_g'], None), (wl['gm_ln_b'], None), (wl['gm_w_s'], None), (wl['gm_b_s'], None),
        (lbl, None), (ones_bd, None),
    ]
    in_specs = [s if s is not None else full(a) for a, s in ins]
    outs = [(QK_WIDTH, BF16), (QK_WIDTH, BF16), (V_WIDTH, BF16), (GM_WIDTH, BF16),
            (3 * HG_KEYS, BF16), (2 * HG_KEYS, F32), (2 * HG_KEYS, F32)]
    return pl.pallas_call(
        functools.partial(_in_proj_kernel, layer=layer, nsub=max(1, tm // IN_SUBTILE)),
        grid=(nt,),
        in_specs=in_specs,
        out_specs=[pl.BlockSpec((tm, w), row) for w, _ in outs],
        out_shape=[jax.ShapeDtypeStruct((T, w), dt) for w, dt in outs],
        compiler_params=pltpu.CompilerParams(dimension_semantics=("arbitrary",),
                                             vmem_limit_bytes=VMEM_LIMIT),
        name="in_proj",
    )(*[a for a, _ in ins])


def _attn_head(hd, q_ref, k_ref, v_ref, s_ref, ck, outs):
    tq, seq = q_ref.shape[0], k_ref.shape[0]
    nck = seq // ck
    sl = slice(hd * HEAD_PAD, (hd + 1) * HEAD_PAD)
    vsl = slice((hd // 2) * 2 * MLA_V, (hd // 2 + 1) * 2 * MLA_V)
    buf = hd % s_ref.shape[0]
    q = q_ref[:, sl]
    mp = None
    for c in range(nck):
        cs = slice(c * ck, (c + 1) * ck)
        s = lax.dot_general(q, k_ref[cs, sl], (((1,), (1,)), ((), ())),
                            preferred_element_type=F32)
        s_ref[buf, :, cs] = s
        for j in range(ck // LANES):
            t = s[:, j * LANES:(j + 1) * LANES]
            mp = t if mp is None else jnp.maximum(mp, t)
        yield
    m = jnp.max(mp, axis=-1, keepdims=True)
    ones = jnp.ones((ck, LANES), BF16)
    acc = jnp.zeros((tq, 2 * MLA_V + LANES), F32)
    for c in range(nck):
        cs = slice(c * ck, (c + 1) * ck)
        p = jnp.exp2(s_ref[buf, :, cs] - m).astype(BF16)
        acc = acc + jnp.dot(p, jnp.concatenate([v_ref[cs, vsl], ones], axis=1),
                            preferred_element_type=F32)
        yield
    outs.append(acc[:, :2 * MLA_V] / acc[:, 2 * MLA_V:])


SUB = 8
NSUB = HG_CHUNK // SUB
ROWS_PER_SUB = tuple(SUB * (NSUB - sb) for sb in range(NSUB))
LHS_ROWS = SUB * sum(ROWS_PER_SUB)


def _chunk_cumsum(x, reverse):
    n = x.shape[0]
    pos = lax.broadcasted_iota(jnp.int32, x.shape, 0) & (HG_CHUNK - 1)
    d = 1
    while d < HG_CHUNK:
        if reverse:
            shifted, ok = pltpu.roll(x, n - d, 0), pos < HG_CHUNK - d
        else:
            shifted, ok = pltpu.roll(x, d, 0), pos >= d
        x = x + jnp.where(ok, shifted, 0.0)
        d *= 2
    return x


def _hgrn_chunk(reverse, c, ones_ref, bmask_ref,
                o_ref, st_ref, qf_ref, vf_ref, b_ref, g_ref, lhs_ref, res_ref):
    ones_bd = ones_ref[...]
    bmask = bmask_ref[...]
    sub_iota = lax.broadcasted_iota(jnp.int32, (SUB, HG_KEYS), 0)
    rows = pl.ds(c * HG_CHUNK, HG_CHUNK)
    b = b_ref[rows, :]
    q = qf_ref[rows, :]
    g = g_ref[rows, :]
    v = vf_ref[rows, :]
    b_end = b[0:1] if reverse else b[HG_CHUNK - 1:HG_CHUNK]

    off = 0
    for sb in range(NSUB):
        if reverse:
            t0, t1 = 0, SUB * (sb + 1)
            diag = slice(t1 - SUB, t1)
        else:
            t0, t1 = SUB * sb, HG_CHUNK
            diag = slice(0, SUB)
        n = t1 - t0
        q_r = q[t0:t1]
        b_r = b[t0:t1]
        for si in range(SUB):
            s = SUB * sb + si
            d = b_r - g[s:s + 1]
            keep = (sub_iota <= si) if reverse else (sub_iota >= si)
            d_diag = jnp.where(keep, d[diag], NEG_BIG)
            if reverse:
                d = jnp.concatenate([d[:n - SUB], d_diag], axis=0) if n > SUB else d_diag
            else:
                d = jnp.concatenate([d_diag, d[SUB:]], axis=0) if n > SUB else d_diag
            lhs_ref[off:off + n, :] = q_r * jnp.exp2(d)
            off += n
            yield
    res_ref[...] = jnp.dot(lhs_ref[...].astype(BF16), ones_bd, preferred_element_type=F32)
    yield

    halves = [slice(pr * LANES, (pr + 1) * LANES) for pr in range(2)]
    st = [st_ref[pr] for pr in range(2)]
    qe = (q * jnp.exp2(b)).astype(BF16)
    o_inter = jnp.concatenate(
        [lax.dot_general(qe[:, hv], st[pr].astype(BF16), (((1,), (1,)), ((), ())),
                         preferred_element_type=F32) for pr, hv in enumerate(halves)], axis=1)
    o_blk = [o_inter[SUB * tb:SUB * (tb + 1)] for tb in range(NSUB)]
    off = 0
    for sb in range(NSUB):
        t0 = 0 if reverse else SUB * sb
        n = SUB * (sb + 1) if reverse else HG_CHUNK - SUB * sb
        for si in range(SUB):
            s = SUB * sb + si
            contrib = res_ref[off:off + n, :] * v[s:s + 1]
            for j in range(n // SUB):
                tb = t0 // SUB + j
                o_blk[tb] = o_blk[tb] + contrib[SUB * j:SUB * (j + 1)]
            off += n
            yield
    o_ref[rows, :] = jnp.concatenate(o_blk, axis=0)

    kdec = jnp.exp2(b_end - g).astype(BF16)
    vb = v.astype(BF16)
    dec = jnp.exp2(b_end)
    for pr, hv in enumerate(halves):
        upd = lax.dot_general(vb[:, hv], kdec[:, hv], (((0,), (0,)), ((), ())),
                              preferred_element_type=F32)
        st_ref[pr] = st[pr] * dec[:, hv] + upd * bmask


SCAN_PIECES = 2 * HG_CHUNK + 1


def _interleave(main, n_main, others, n_other):
    done = 0
    for i, _ in enumerate(main):
        target = -(-(i + 1) * n_other // n_main)
        while done < target:
            for g in others:
                next(g, None)
            done += 1
    for g in others:
        for _ in g:
            pass

def _mix_kernel(q_ref, k_ref, v_ref,
                qf_in, vf_in, lff_in, kkf_in, qb_in, vb_in, lfb_in, kkb_in, ones_ref, bmask_ref,
                oa_out, of_out, ob_out,
                s_ref, st_ref, qf_ref, vf_ref, b_ref, g_ref, lhs_ref, res_ref, *, ck, nh, nq, nchunk):
    @pl.when(pl.program_id(1) * nq + pl.program_id(2) == 0)
    def _():
        st_ref[...] = jnp.zeros_like(st_ref)

    for d, (q_in, v_in, lf_in, lk_in) in enumerate(((qf_in, vf_in, lff_in, kkf_in),
                                                     (qb_in, vb_in, lfb_in, kkb_in))):
        qf_ref[d] = q_in[...].astype(F32)
        vf_ref[d] = v_in[...].astype(F32)
        b = _chunk_cumsum(lf_in[...], reverse=bool(d))
        b_ref[d] = b
        g_ref[d] = b - lk_in[...]

    def scan_stream(reverse):
        d = int(reverse)
        o_out = ob_out if reverse else of_out
        for ci in range(nchunk):
            yield from _hgrn_chunk(reverse, nchunk - 1 - ci if reverse else ci, ones_ref,
                                   bmask_ref, o_out, st_ref.at[d], qf_ref.at[d], vf_ref.at[d],
                                   b_ref.at[d], g_ref.at[d], lhs_ref.at[d], res_ref.at[d])

    def attn_stream(outs):
        grp = s_ref.shape[0]
        for h0 in range(0, nh, grp):
            res = [[] for _ in range(grp)]
            live = [_attn_head(h0 + i, q_ref, k_ref, v_ref, s_ref, ck, res[i]) for i in range(grp)]
            while live:
                for gen in list(live):
                    if next(gen, "done") == "done":
                        live.remove(gen)
                    else:
                        yield
            for r in res:
                outs.extend(r)

    outs = []
    n_attn = nh * 2 * (k_ref.shape[0] // ck)
    n_scan = nchunk * SCAN_PIECES
    _interleave(attn_stream(outs), n_attn, [scan_stream(False), scan_stream(True)], n_scan)
    lane = lax.broadcasted_iota(jnp.int32, outs[0].shape, 1)
    for pr in range(nh // 2):
        oa_out[:, pr * 2 * MLA_V:(pr + 1) * 2 * MLA_V] = jnp.where(
            lane < MLA_V, outs[2 * pr], outs[2 * pr + 1]).astype(BF16)


def _mixers(q, k, v, hqig, logf, kk, consts, *, batch, seq, tq, ck, nh, grp):
    T = q.shape[0]
    nq = seq // tq
    nsteps = (MLA_HEADS // nh) * nq
    tb = seq // nsteps
    fwd = lambda col: (lambda b, h, i: (b * nsteps + h * nq + i, col))
    bwd = lambda col: (lambda b, h, i: (b * nsteps + nsteps - 1 - (h * nq + i), col))
    blk = lambda m: pl.BlockSpec((tb, HG_KEYS), m)
    const = lambda a: pl.BlockSpec(a.shape, lambda b, h, i: (0, 0))
    ones_bd, bmask = consts
    kv_bytes = seq * nh * (HEAD_PAD + MLA_V) * 2
    s_bytes = grp * tq * seq * 4
    kv_mode = {} if s_bytes + 2 * kv_bytes <= KV_DOUBLE_BUFFER_BUDGET else dict(pipeline_mode=pl.Buffered(1))
    scratch = [pltpu.VMEM((grp, tq, seq), F32),
               pltpu.VMEM((2, HG_HEADS // 2, LANES, LANES), F32),
               pltpu.VMEM((2, tb, HG_KEYS), F32), pltpu.VMEM((2, tb, HG_KEYS), F32),
               pltpu.VMEM((2, tb, HG_KEYS), F32), pltpu.VMEM((2, tb, HG_KEYS), F32),
               pltpu.VMEM((2, LHS_ROWS, HG_KEYS), F32), pltpu.VMEM((2, LHS_ROWS, HG_KEYS), F32)]
    return pl.pallas_call(
        functools.partial(_mix_kernel, ck=ck, nh=nh, nq=nq, nchunk=tb // HG_CHUNK),
        grid=(batch, MLA_HEADS // nh, nq),
        in_specs=[
            pl.BlockSpec((tq, nh * HEAD_PAD), lambda b, h, i: (b * nq + i, h)),
            pl.BlockSpec((seq, nh * HEAD_PAD), lambda b, h, i: (b, h), **kv_mode),
            pl.BlockSpec((seq, nh * MLA_V), lambda b, h, i: (b, h), **kv_mode),
            blk(fwd(0)), blk(fwd(1)), blk(fwd(0)), blk(fwd(0)),
            blk(bwd(0)), blk(bwd(1)), blk(bwd(1)), blk(bwd(1)),
            const(ones_bd), const(bmask)],
        out_specs=[pl.BlockSpec((tq, nh * MLA_V), lambda b, h, i: (b * nq + i, h)),
                   blk(fwd(0)), blk(bwd(0))],
        out_shape=[jax.ShapeDtypeStruct((T, V_WIDTH), BF16),
                   jax.ShapeDtypeStruct((T, HG_WIDTH), F32), jax.ShapeDtypeStruct((T, HG_WIDTH), F32)],
        scratch_shapes=scratch,
        compiler_params=pltpu.CompilerParams(
            dimension_semantics=("arbitrary", "arbitrary", "arbitrary"),
            vmem_limit_bytes=VMEM_LIMIT),
        name="mixers",
    )(q, k, v, hqig, hqig, logf, kk, hqig, hqig, logf, kk, ones_bd, bmask)


FF_BLOCK = 1024
FFN_SUBTILE = 512


def _out_ffn_kernel(x_ref, oa_ref, ob_ref, of_ref, obw_ref, g_ref, go_ref, ones_ref,
                    wo_ref, gpm_ref, gpf_ref, w1_ref, w2_ref, gpo_ref, y_ref, *, nsub):
    n = x_ref.shape[0] // nsub
    for r in range(nsub):
        rs = slice(r * n, (r + 1) * n)
        o = of_ref[rs, :] + obw_ref[rs, :]
        ms = _seg_sum(o * o, ones_ref[...]) * (1.0 / 64)
        gate = g_ref[rs, :].astype(F32)
        oc = o * lax.rsqrt(ms + EPS) * go_ref[...] * (gate * jax.nn.sigmoid(gate))
        mixed = jnp.concatenate([oa_ref[rs, :], ob_ref[rs, :], oc.astype(BF16)], axis=1)
        m = jnp.dot(mixed, wo_ref[...], preferred_element_type=F32)
        x1 = x_ref[rs, :] + _rms(m, gpm_ref[...])
        h = _rms(x1, gpf_ref[...]).astype(BF16)
        acc = jnp.zeros(x1.shape, F32)
        for c in range(D_FF // FF_BLOCK):
            cs = slice(c * FF_BLOCK, (c + 1) * FF_BLOCK)
            a = jnp.maximum(jnp.dot(h, w1_ref[:, cs], preferred_element_type=F32), 0.0)
            acc = acc + jnp.dot((a * a).astype(BF16), w2_ref[cs, :], preferred_element_type=F32)
        y_ref[rs, :] = x1 + _rms(acc, gpo_ref[...])


def _out_ffn(x2d, oa, ob, of, obw, hqig, wl, ones_bd, *, tm):
    T = x2d.shape[0]
    row = lambda i: (i, 0)
    const = lambda a: pl.BlockSpec(a.shape, lambda i: (0, 0), pipeline_mode=pl.Buffered(1))
    ws = [wl['hg_g_o'], ones_bd, wl['w_out'], wl['g_post_mix'], wl['g_pre_ffn'],
          wl['w_ff1'], wl['w_ff2'], wl['g_post_ffn']]
    return pl.pallas_call(
        functools.partial(_out_ffn_kernel, nsub=max(1, tm // FFN_SUBTILE)),
        grid=(T // tm,),
        in_specs=[pl.BlockSpec((tm, D_MODEL), row), pl.BlockSpec((tm, V_WIDTH), row),
                  pl.BlockSpec((tm, GM_WIDTH), row), pl.BlockSpec((tm, HG_WIDTH), row),
                  pl.BlockSpec((tm, HG_WIDTH), row),
                  pl.BlockSpec((tm, HG_WIDTH), lambda i: (i, 2))] + [const(a) for a in ws],
        out_specs=pl.BlockSpec((tm, D_MODEL), row),
        out_shape=jax.ShapeDtypeStruct((T, D_MODEL), F32),
        compiler_params=pltpu.CompilerParams(dimension_semantics=("arbitrary",),
                                             vmem_limit_bytes=VMEM_LIMIT),
        name="out_ffn",
    )(x2d, oa, ob, of, obw, hqig, *ws)


def _rope_tables(seq):
    half = MLA_ROPE // 2
    inv = ROPE_BASE ** (-(jnp.arange(half, dtype=F32) / half))
    ang = jnp.arange(seq, dtype=F32)[:, None] * inv[None, :]
    cos, sin = jnp.cos(ang), jnp.sin(ang)
    pad = jnp.zeros((seq, HEAD_PAD - MLA_NOPE - MLA_ROPE), F32)
    ctab = jnp.concatenate([jnp.ones((seq, MLA_NOPE), F32), cos, cos, pad], axis=1)
    stab = jnp.concatenate([jnp.zeros((seq, MLA_NOPE), F32), -sin, sin, pad], axis=1)
    return ctab, stab


def _pack_weights(l, g_pre_mix, w_in, g_q_a, w_uq, g_kv_a, w_ukv, gm_ln_g, gm_ln_b, gm_w_s, gm_b_s,
                  hg_g_o, w_out, g_post_mix, g_pre_ffn, w_ff1, w_ff2, g_post_ffn):
    half = MLA_ROPE // 2
    swap = lambda a: jnp.concatenate([a[..., half:], a[..., :half]], axis=-1)
    row = lambda a: a[l][None, :]
    zpad = lambda n, k: jnp.zeros((n, k), F32)

    wi = w_in[l]
    o = np.cumsum((0, MLA_Q_RANK, MLA_KV_RANK, MLA_ROPE, GM_WIDTH, GM_WIDTH,
                   HG_KEYS, HG_KEYS, HG_KEYS, HG_WIDTH, HG_WIDTH))
    kr = wi[:, o[2]:o[3]]
    tail = HEAD_PAD - MLA_NOPE - MLA_ROPE
    kpe = jnp.concatenate([zpad(D_MODEL, MLA_NOPE), kr, zpad(D_MODEL, tail)], axis=1)
    kps = jnp.concatenate([zpad(D_MODEL, MLA_NOPE), swap(kr), zpad(D_MODEL, tail)], axis=1)
    w_in_p = jnp.concatenate([wi[:, o[0]:o[2]], kpe, kps, wi[:, o[3]:]], axis=1)

    wq = w_uq[l].reshape(MLA_Q_RANK, MLA_HEADS, MLA_NOPE + MLA_ROPE)
    qn, qr = wq[..., :MLA_NOPE], wq[..., MLA_NOPE:]
    zq = jnp.zeros((MLA_Q_RANK, MLA_HEADS, tail), F32)
    w_q = jnp.concatenate([qn, qr, zq], axis=-1).reshape(MLA_Q_RANK, QK_WIDTH)
    w_qs = jnp.concatenate([jnp.zeros_like(qn), swap(qr), zq], axis=-1).reshape(MLA_Q_RANK, QK_WIDTH)

    wkv = w_ukv[l].reshape(MLA_KV_RANK, MLA_HEADS, MLA_NOPE + MLA_V)
    kn, vv = wkv[..., :MLA_NOPE], wkv[..., MLA_NOPE:]
    zk = jnp.zeros((MLA_KV_RANK, MLA_HEADS, HEAD_PAD - MLA_NOPE), F32)
    w_k = jnp.concatenate([kn, zk], axis=-1).reshape(MLA_KV_RANK, QK_WIDTH)
    w_v = vv.reshape(MLA_KV_RANK, V_WIDTH)

    return dict(
        g_pre_mix=row(g_pre_mix), w_in=w_in_p.astype(BF16), g_q_a=row(g_q_a), w_q=w_q.astype(BF16),
        w_qs=w_qs.astype(BF16), g_kv_a=row(g_kv_a), w_k=w_k.astype(BF16), w_v=w_v.astype(BF16),
        gm_ln_g=row(gm_ln_g), gm_ln_b=row(gm_ln_b), gm_w_s=gm_w_s[l].astype(BF16),
        gm_b_s=jnp.broadcast_to(gm_b_s[l][:, None], (GM_CHUNK, GM_WIDTH)),
        hg_g_o=row(hg_g_o), w_out=w_out[l].astype(BF16), g_post_mix=row(g_post_mix),
        g_pre_ffn=row(g_pre_ffn), w_ff1=w_ff1[l].astype(BF16), w_ff2=w_ff2[l].astype(BF16),
        g_post_ffn=row(g_post_ffn))


def _scan_consts():
    k = np.arange(HG_KEYS)
    bd = (k[:, None] // HG_DK) == (k[None, :] // HG_DK)
    return jnp.asarray(bd, BF16), jnp.asarray(bd[:LANES, :LANES], F32)


class Tiles(NamedTuple):
    tm: int
    tq: int
    ck: int
    nh: int
    grp: int


def _tiles(seq):
    per_row = seq * 4
    tq = next(t for t in (512, 256, 128) if 2 * t * per_row <= SCORE_SCRATCH_BUDGET)
    grp = min(4, SCORE_SCRATCH_BUDGET // (tq * per_row))
    ck = SCORE_CHUNK_ELEMS // tq
    return Tiles(tm=2 * IN_SUBTILE, tq=tq, ck=ck, nh=MLA_HEADS, grp=grp)


def _trunk(x, layers, lbl, tiles=None):
    batch, seq, _ = x.shape
    t = tiles or _tiles(seq)
    x2d = x.reshape(batch * seq, D_MODEL)
    tabs = _rope_tables(seq)
    consts = _scan_consts()
    ones_bd = consts[0]
    for l, wl in enumerate(layers):
        q, k, v, ob, hqig, logf, kk = _in_proj(x2d, wl, tabs, lbl, ones_bd, layer=l, seq=seq, tm=t.tm)
        oa, of, obw = _mixers(q, k, v, hqig, logf, kk, consts, batch=batch, seq=seq, tq=t.tq, ck=t.ck,
                              nh=t.nh, grp=t.grp)
        x2d = _out_ffn(x2d, oa, ob, of, obw, hqig, wl, ones_bd, tm=t.tm)
    return x2d.reshape(batch, seq, D_MODEL)


def kernel(x_prompt, x_sample, hg_lb_logits, g_pre_mix, w_in, g_q_a, w_uq, g_kv_a, w_ukv, gm_ln_g, gm_ln_b,
           gm_w_s, gm_b_s, hg_g_o, w_out, g_post_mix, g_pre_ffn, w_ff1, w_ff2, g_post_ffn):
    depth = w_in.shape[0]
    layers = [_pack_weights(l, g_pre_mix, w_in, g_q_a, w_uq, g_kv_a, w_ukv, gm_ln_g, gm_ln_b, gm_w_s,
                            gm_b_s, hg_g_o, w_out, g_post_mix, g_pre_ffn, w_ff1, w_ff2, g_post_ffn)
              for l in range(depth)]
    lbl = hg_lb_logits.reshape(depth, 2 * HG_KEYS)
    return (_trunk(x_prompt, layers, lbl), _trunk(x_sample, layers, lbl))
```
